```python
import jax, jax.numpy as jnp
from jax import lax
import numpy as np

D_MODEL = 2048
BATCH = 32
SEQ = 256
DEPTH = 4
DEC_BATCH = 8
DEC_SEQ = 1024
PAST_LEN = 256

GRID_W = 64
HD = 128
MIX_WIDTH = 1024
N_BRANCH = 4
POOL_GROUPS = 4
POOL_GROUP_DIM = MIX_WIDTH // POOL_GROUPS
POOL_WINDOWS = (2, 4, 8, 16)
B_HEADS = 8
B_KV_HEADS = 2
GQA_GROUP = B_HEADS // B_KV_HEADS
B_WINDOW = 128
B_BLOCK = 128
C_HEADS = 8
NA_KH = 8
NA_KW = 16
SSM_HEADS = 16
SSM_HEADDIM = 64
SSM_INNER = SSM_HEADS * SSM_HEADDIM
SSM_GROUPS = 2
SSM_STATE = 64
SSM_CONV = 5
SSM_CHUNK = 128
SSM_XBC = SSM_INNER + 2 * SSM_GROUPS * SSM_STATE
D_FF = 4 * D_MODEL
Q_BLOCK = 128
ROPE_BASE = 10000.0
LN_EPS = 1e-5
RMS_EPS = 1e-6
NEG_INF = -1e30
ALPHA = (2 * DEPTH) ** 0.25
BETA = (8 * DEPTH) ** -0.25
IN_SIZES = (MIX_WIDTH, B_HEADS * HD, B_KV_HEADS * HD, B_KV_HEADS * HD, C_HEADS * HD, C_HEADS * HD, C_HEADS * HD, SSM_INNER, SSM_XBC, 2 * SSM_HEADS, N_BRANCH * D_MODEL)
D_IN = sum(IN_SIZES)

kernel_name = 'hybrid_pool_swa_natten_ssd_diffusion_step'


def layer_norm(x, g, b):
    xf = x.astype(jnp.float32)
    mu = jnp.mean(xf, axis=-1, keepdims=True)
    var = jnp.mean(jnp.square(xf - mu), axis=-1, keepdims=True)
    return ((xf - mu) * lax.rsqrt(var + LN_EPS)).astype(x.dtype) * g + b


def rms_norm(x, g):
    xf = x.astype(jnp.float32)
    return (xf * lax.rsqrt(jnp.mean(jnp.square(xf), axis=-1, keepdims=True) + RMS_EPS)).astype(x.dtype) * g


def modulation(cvec, w_ada, b_ada):
    return jnp.split(jax.nn.silu(cvec) @ w_ada + b_ada, 6, axis=-1)


def rope_2d(x):
    L = x.shape[1]
    t = jnp.arange(L)
    half = HD // 2
    quarter = half // 2
    freqs = ROPE_BASE ** (-jnp.arange(quarter, dtype=jnp.float32) / quarter)

    def rotate(xa, pos):
        ang = pos.astype(jnp.float32)[:, None] * freqs[None, :]
        cos = jnp.cos(ang)[None, :, None, :].astype(xa.dtype)
        sin = jnp.sin(ang)[None, :, None, :].astype(xa.dtype)
        x1, x2 = xa[..., :quarter], xa[..., quarter:]
        return jnp.concatenate([x1 * cos - x2 * sin, x1 * sin + x2 * cos], axis=-1)

    return jnp.concatenate([rotate(x[..., :half], t // GRID_W), rotate(x[..., half:], t % GRID_W)], axis=-1)


def pool_mix(a, w_pool, pool_scale):
    bsz, L, _ = a.shape
    ag = a.reshape(bsz, L, POOL_GROUPS, POOL_GROUP_DIM)
    cs = jnp.pad(jnp.cumsum(ag.astype(jnp.float32), axis=1), ((0, 0), (1, 0), (0, 0), (0, 0)))
    pos = jnp.arange(L)
    pooled = []
    for g, w in enumerate(POOL_WINDOWS):
        lo = jnp.clip(pos - w // 2, 0, L)
        hi = jnp.clip(pos + w - w // 2, 0, L)
        csg = cs[:, :, g]
        cnt = (hi - lo).astype(jnp.float32)[None, :, None]
        pooled.append((csg[:, hi] - csg[:, lo]) / cnt)
    pooled = jnp.stack(pooled, axis=2).astype(a.dtype) - ag
    y = jnp.einsum('blgc,gcd->blgd', pooled, w_pool)
    return y.reshape(bsz, L, MIX_WIDTH) * pool_scale


def dense_ctx_attention(q, k, v, sink):
    bsz, s_len, kvh, grp, _ = q.shape
    nb = s_len // Q_BLOCK
    qb = jnp.moveaxis(q.reshape(bsz, nb, Q_BLOCK, kvh, grp, HD), 1, 0)
    scale = HD ** -0.5

    def block(q_blk):
        s = jnp.einsum('bqhgd,bkhd->bhgqk', q_blk, k).astype(jnp.float32) * scale
        if sink is not None:
            s_sink = jnp.broadcast_to(sink.astype(jnp.float32)[None, :, :, None, None], s.shape[:-1] + (1,))
            s = jnp.concatenate([s, s_sink], axis=-1)
        p = jax.nn.softmax(s, axis=-1)[..., :s_len].astype(v.dtype)
        return jnp.einsum('bhgqk,bkhd->bqhgd', p, v)

    o = lax.map(block, qb)
    return jnp.moveaxis(o, 0, 1).reshape(bsz, s_len, kvh * grp * HD)


def band_gqa_latent(q, k, v, ck, cv, sink):
    bsz, L = q.shape[:2]
    nb = L // B_BLOCK
    span = B_BLOCK + 2 * B_WINDOW
    kp = jnp.pad(k, ((0, 0), (B_WINDOW, B_WINDOW), (0, 0), (0, 0)))
    vp = jnp.pad(v, ((0, 0), (B_WINDOW, B_WINDOW), (0, 0), (0, 0)))
    idx = jnp.arange(nb)[:, None] * B_BLOCK + jnp.arange(span)[None, :]
    kb, vb = kp[:, idx], vp[:, idx]
    qb = q.reshape(bsz, nb, B_BLOCK, B_KV_HEADS, GQA_GROUP, HD)
    scale = HD ** -0.5
    s_loc = jnp.einsum('bnqhgd,bnkhd->bnhgqk', qb, kb).astype(jnp.float32) * scale
    kpos = idx - B_WINDOW
    qpos = jnp.arange(L).reshape(nb, B_BLOCK)
    valid = (kpos[:, None, :] >= 0) & (kpos[:, None, :] < L) & (jnp.abs(qpos[:, :, None] - kpos[:, None, :]) <= B_WINDOW)
    s_loc = jnp.where(valid[None, :, None, None, :, :], s_loc, NEG_INF)
    s_ctx = jnp.einsum('bnqhgd,bphd->bnhgqp', qb, ck).astype(jnp.float32) * scale
    s_sink = jnp.broadcast_to(sink.astype(jnp.float32).reshape(1, 1, B_KV_HEADS, GQA_GROUP, 1, 1), s_loc.shape[:-1] + (1,))
    p = jax.nn.softmax(jnp.concatenate([s_loc, s_ctx, s_sink], axis=-1), axis=-1).astype(v.dtype)
    n_ctx = ck.shape[1]
    o = jnp.einsum('bnhgqk,bnkhd->bnqhgd', p[..., :span], vb) + jnp.einsum('bnhgqp,bphd->bnqhgd', p[..., span:span + n_ctx], cv)
    return o.reshape(bsz, L, B_HEADS * HD)


def neighbourhood_attn_latent(q, k, v, ck, cv, rpb):
    bsz, L = q.shape[:2]
    rows = L // GRID_W
    kh = min(NA_KH, rows)
    r = jnp.arange(rows)
    r0 = jnp.clip(r - kh // 2, 0, rows - kh)
    key_rows = r0[:, None] + jnp.arange(kh)[None, :]
    col = jnp.arange(GRID_W)
    c0 = jnp.clip(col - NA_KW // 2, 0, GRID_W - NA_KW)
    qg = q.reshape(bsz, rows, GRID_W, C_HEADS, HD)
    kg = k.reshape(bsz, rows, GRID_W, C_HEADS, HD)[:, key_rows]
    vg = v.reshape(bsz, rows, GRID_W, C_HEADS, HD)[:, key_rows]
    scale = HD ** -0.5
    s_loc = jnp.einsum('brqhd,brjkhd->brhqjk', qg, kg).astype(jnp.float32) * scale
    dr_i = key_rows - r[:, None] + (NA_KH - 1)
    dc_i = jnp.clip(col[None, :] - col[:, None], -(NA_KW - 1), NA_KW - 1) + (NA_KW - 1)
    bias = rpb[:, dr_i[:, None, :, None], dc_i[None, :, None, :]]
    s_loc = s_loc + jnp.moveaxis(bias, 0, 1).astype(jnp.float32)[None]
    col_ok = (col[None, :] >= c0[:, None]) & (col[None, :] < c0[:, None] + NA_KW)
    s_loc = jnp.where(col_ok[None, None, None, :, None, :], s_loc, NEG_INF).reshape(bsz, rows, C_HEADS, GRID_W, kh * GRID_W)
    s_ctx = jnp.einsum('brqhd,bphd->brhqp', qg, ck).astype(jnp.float32) * scale
    p = jax.nn.softmax(jnp.concatenate([s_loc, s_ctx], axis=-1), axis=-1).astype(v.dtype)
    p_loc = p[..., :kh * GRID_W].reshape(bsz, rows, C_HEADS, GRID_W, kh, GRID_W)
    o = jnp.einsum('brhqjk,brjkhd->brqhd', p_loc, vg) + jnp.einsum('brhqp,bphd->brqhd', p[..., kh * GRID_W:], cv)
    return o.reshape(bsz, L, C_HEADS * HD)


def dw_conv_centred(x, w, b):
    pad = SSM_CONV // 2
    y = lax.conv_general_dilated(x, w[:, None, :], window_strides=(1,), padding=[(pad, pad)],
                                 dimension_numbers=('NWC', 'WIO', 'NWC'), feature_group_count=x.shape[-1])
    return y + b


def ssd_scan(x, dt, a, bm, cm, h0):
    f32 = jnp.float32
    bsz, L, H, P = x.shape
    nc, Q = L // SSM_CHUNK, SSM_CHUNK
    rep = SSM_HEADS // SSM_GROUPS
    xc = x.astype(f32).reshape(bsz, nc, Q, H, P)
    bc = jnp.repeat(bm.astype(f32), rep, axis=2).reshape(bsz, nc, Q, H, SSM_STATE)
    cc = jnp.repeat(cm.astype(f32), rep, axis=2).reshape(bsz, nc, Q, H, SSM_STATE)
    dtc = dt.astype(f32).reshape(bsz, nc, Q, H)
    cum = jnp.cumsum(dtc * a, axis=2)
    lower = jnp.tril(jnp.ones((Q, Q), dtype=bool))[None, None, :, :, None]
    seg = jnp.exp(jnp.where(lower, cum[:, :, :, None, :] - cum[:, :, None, :, :], NEG_INF))
    w_intra = jnp.einsum('bcihn,bcjhn->bcijh', cc, bc) * seg * dtc[:, :, None, :, :]
    y_intra = jnp.einsum('bcijh,bcjhp->bcihp', w_intra, xc)
    w_state = jnp.exp(cum[:, :, -1:, :] - cum) * dtc
    states = jnp.einsum('bcjh,bcjhn,bcjhp->bchpn', w_state, bc, xc)
    chunk_decay = jnp.exp(cum[:, :, -1, :])

    def step(h, inp):
        st, dec = inp
        return h * dec[:, :, None, None] + st, h

    h_fin, h_prev = lax.scan(step, h0.astype(f32), (jnp.moveaxis(states, 1, 0), jnp.moveaxis(chunk_decay, 1, 0)))
    h_prev = jnp.moveaxis(h_prev, 0, 1)
    y_inter = jnp.einsum('bcihn,bchpn->bcihp', cc, h_prev) * jnp.exp(cum)[..., None]
    y = (y_intra + y_inter).reshape(bsz, L, H, P)
    return y.astype(x.dtype), h_fin.astype(h0.dtype)


def ssd_mixer(z, xbc, dtr, conv_w, conv_b, a_log, dt_bias, d_skip, norm_g, h0_pair):
    bsz, L, _ = z.shape
    xbc = jax.nn.silu(dw_conv_centred(xbc, conv_w, conv_b))
    xs, bm, cm = jnp.split(xbc, [SSM_INNER, SSM_INNER + SSM_GROUPS * SSM_STATE], axis=-1)
    xs = xs.reshape(bsz, L, SSM_HEADS, SSM_HEADDIM)
    bm = bm.reshape(bsz, L, SSM_GROUPS, SSM_STATE)
    cm = cm.reshape(bsz, L, SSM_GROUPS, SSM_STATE)
    dt = jax.nn.softplus(dtr.reshape(bsz, L, 2, SSM_HEADS) + dt_bias)
    a = -jnp.exp(a_log.astype(jnp.float32))
    y_f, h_f = ssd_scan(xs, dt[:, :, 0], a[0], bm, cm, h0_pair[0])
    y_b, h_b = ssd_scan(xs[:, ::-1], dt[:, ::-1, 1], a[1], bm[:, ::-1], cm[:, ::-1], h0_pair[1])
    y = y_f + y_b[:, ::-1] + xs * d_skip[:, None]
    y = y.reshape(bsz, L, SSM_INNER) * jax.nn.silu(z)
    return rms_norm(y, norm_g), jnp.stack([h_f, h_b], axis=1)


def trunk_layer(x, mod, ctx_kv, h0_pair, latent, w_in, w_pool, pool_scale, sink, rpb, conv_w, conv_b,
                a_log, dt_bias, d_skip, ssm_norm, w_branch, w_o, ln1_g, ln1_b, w_mlp1, w_mlp2, ln2_g, ln2_b):
    shift1, scale1, gate1, shift2, scale2, gate2 = mod
    bsz, L, _ = x.shape
    h = x * (1 + scale1) + shift1
    splits = np.cumsum(IN_SIZES)[:-1].tolist()
    a, qb, kb, vb, qc, kc, vc, z, xbc, dtr, gates = jnp.split(h @ w_in, splits, axis=-1)
    y_a = pool_mix(a, w_pool, pool_scale)
    qb = qb.reshape(bsz, L, B_HEADS, HD)
    kb = kb.reshape(bsz, L, B_KV_HEADS, HD)
    vb = vb.reshape(bsz, L, B_KV_HEADS, HD)
    qc = qc.reshape(bsz, L, C_HEADS, HD)
    kc = kc.reshape(bsz, L, C_HEADS, HD)
    vc = vc.reshape(bsz, L, C_HEADS, HD)
    if latent:
        ck_b, cv_b, ck_c, cv_c = ctx_kv
        kb = rope_2d(kb)
        y_b = band_gqa_latent(rope_2d(qb).reshape(bsz, L, B_KV_HEADS, GQA_GROUP, HD), kb, vb, ck_b, cv_b, sink)
        y_c = neighbourhood_attn_latent(qc, kc, vc, ck_c, cv_c, rpb)
    else:
        y_b = dense_ctx_attention(qb.reshape(bsz, L, B_KV_HEADS, GQA_GROUP, HD), kb, vb, sink.reshape(B_KV_HEADS, GQA_GROUP))
        y_c = dense_ctx_attention(qc[:, :, :, None, :], kc, vc, None)
    y_d, h_fin = ssd_mixer(z, xbc, dtr, conv_w, conv_b, a_log, dt_bias, d_skip, ssm_norm, h0_pair)
    branches = jnp.stack([y_a, y_b, y_c, y_d], axis=2)
    proj = jnp.einsum('blnm,nmd->blnd', branches, w_branch)
    g = jax.nn.sigmoid(gates.reshape(bsz, L, N_BRANCH, D_MODEL))
    merged = jnp.sum(g * proj, axis=2) @ w_o
    x = layer_norm(ALPHA * x + gate1 * merged, ln1_g, ln1_b)
    h2 = x * (1 + scale2) + shift2
    ff = jnp.square(jax.nn.relu(h2 @ w_mlp1)) @ w_mlp2
    x = layer_norm(ALPHA * x + gate2 * ff, ln2_g, ln2_b)
    return x, (kb, vb, kc, vc, h_fin)


def setup_inputs(seed: int = 0) -> dict:
    key = jax.random.key(seed)
    ks = jax.random.split(key, 32)
    f32 = jnp.float32

    def nrm(k, shape, s):
        return jax.random.normal(k, shape, f32) * s

    return {
        'x_prompt': nrm(ks[0], (BATCH, SEQ, D_MODEL), 1.0),
        'x_sample': nrm(ks[1], (DEC_BATCH, DEC_SEQ, D_MODEL), 1.0),
        'cache_b_k': nrm(ks[2], (DEC_BATCH, DEPTH, PAST_LEN, B_KV_HEADS, HD), 1.0),
        'cache_b_v': nrm(ks[3], (DEC_BATCH, DEPTH, PAST_LEN, B_KV_HEADS, HD), 1.0),
        'cache_c_k': nrm(ks[4], (DEC_BATCH, DEPTH, PAST_LEN, C_HEADS, HD), 1.0),
        'cache_c_v': nrm(ks[5], (DEC_BATCH, DEPTH, PAST_LEN, C_HEADS, HD), 1.0),
        'state_ssm': nrm(ks[6], (DEC_BATCH, DEPTH, 2, SSM_HEADS, SSM_HEADDIM, SSM_STATE), 0.1),
        'c': nrm(ks[7], (DEC_BATCH, D_MODEL), 1.0),
        'c_ctx': nrm(ks[8], (D_MODEL,), 1.0),
        'w_ada': nrm(ks[9], (DEPTH, D_MODEL, 6 * D_MODEL), 0.5 * D_MODEL ** -0.5),
        'b_ada': nrm(ks[10], (DEPTH, 6 * D_MODEL), 0.02),
        'w_in': nrm(ks[11], (DEPTH, D_MODEL, D_IN), D_MODEL ** -0.5),
        'w_pool': nrm(ks[12], (DEPTH, POOL_GROUPS, POOL_GROUP_DIM, POOL_GROUP_DIM), POOL_GROUP_DIM ** -0.5),
        'pool_scale': 1.0 + nrm(ks[13], (DEPTH, MIX_WIDTH), 0.02),
        'attn_sink': nrm(ks[14], (DEPTH, B_HEADS), 0.5),
        'na_rpb': nrm(ks[15], (DEPTH, C_HEADS, 2 * NA_KH - 1, 2 * NA_KW - 1), 0.1),
        'conv_w': nrm(ks[16], (DEPTH, SSM_CONV, SSM_XBC), SSM_CONV ** -0.5),
        'conv_b': nrm(ks[17], (DEPTH, SSM_XBC), 0.02),
        'a_log': jnp.log(jax.random.uniform(ks[18], (DEPTH, 2, SSM_HEADS), f32, 1.0, 16.0)),
        'dt_bias': jax.random.uniform(ks[19], (DEPTH, 2, SSM_HEADS), f32, -6.0, -2.0),
        'd_skip': 1.0 + nrm(ks[20], (DEPTH, SSM_HEADS), 0.1),
        'ssm_norm': 1.0 + nrm(ks[21], (DEPTH, SSM_INNER), 0.02),
        'w_branch': nrm(ks[22], (DEPTH, N_BRANCH, MIX_WIDTH, D_MODEL), BETA * MIX_WIDTH ** -0.5),
        'w_o': nrm(ks[23], (DEPTH, D_MODEL, D_MODEL), BETA * D_MODEL ** -0.5),
        'ln1_g': 1.0 + nrm(ks[24], (DEPTH, D_MODEL), 0.02),
        'ln1_b': nrm(ks[25], (DEPTH, D_MODEL), 0.02),
        'w_mlp1': nrm(ks[26], (DEPTH, D_MODEL, D_FF), D_MODEL ** -0.5),
        'w_mlp2': nrm(ks[27], (DEPTH, D_FF, D_MODEL), BETA * D_FF ** -0.5),
        'ln2_g': 1.0 + nrm(ks[28], (DEPTH, D_MODEL), 0.02),
        'ln2_b': nrm(ks[29], (DEPTH, D_MODEL), 0.02),
    }


def reference(x_prompt, x_sample, cache_b_k, cache_b_v, cache_c_k, cache_c_v, state_ssm, c, c_ctx,
              w_ada, b_ada, w_in, w_pool, pool_scale, attn_sink, na_rpb, conv_w, conv_b, a_log, dt_bias,
              d_skip, ssm_norm, w_branch, w_o, ln1_g, ln1_b, w_mlp1, w_mlp2, ln2_g, ln2_b):
    y_prompt = x_prompt
    y_sample = x_sample
    h0 = jnp.zeros((x_prompt.shape[0], SSM_HEADS, SSM_HEADDIM, SSM_STATE), x_prompt.dtype)
    bk, bv, ckk, cvv, hs = [], [], [], [], []
    for l in range(DEPTH):
        lw = (w_in[l], w_pool[l], pool_scale[l], attn_sink[l], na_rpb[l], conv_w[l], conv_b[l], a_log[l],
              dt_bias[l], d_skip[l], ssm_norm[l], w_branch[l], w_o[l], ln1_g[l], ln1_b[l], w_mlp1[l],
              w_mlp2[l], ln2_g[l], ln2_b[l])
        mod_ctx = modulation(c_ctx, w_ada[l], b_ada[l])
        y_prompt, (kb, vb, kc, vc, h_ctx) = trunk_layer(y_prompt, mod_ctx, None, (h0, h0), False, *lw)
        bk.append(kb)
        bv.append(vb)
        ckk.append(kc)
        cvv.append(vc)
        hs.append(h_ctx)
        mod_lat = modulation(c[:, None, :], w_ada[l], b_ada[l])
        ctx_kv = (cache_b_k[:, l], cache_b_v[:, l], cache_c_k[:, l], cache_c_v[:, l])
        y_sample, _ = trunk_layer(y_sample, mod_lat, ctx_kv, (state_ssm[:, l, 0], state_ssm[:, l, 1]), True, *lw)
    new_cache_b_k = jnp.stack(bk, axis=1)
    new_cache_b_v = jnp.stack(bv, axis=1)
    new_cache_c_k = jnp.stack(ckk, axis=1)
    new_cache_c_v = jnp.stack(cvv, axis=1)
    new_state_ssm = jnp.stack(hs, axis=1)
    return (y_prompt, y_sample, new_cache_b_k, new_cache_b_v, new_cache_c_k, new_cache_c_v, new_state_ssm)
```

```python
import functools

import jax
import jax.numpy as jnp
import numpy as np
from jax import lax
from jax.experimental import pallas as pl
from jax.experimental.pallas import tpu as pltpu

D_MODEL = 2048
BATCH = 32
SEQ = 256
DEPTH = 4
DEC_BATCH = 8
DEC_SEQ = 1024
PAST_LEN = 256
GRID_W = 64
HD = 128
MIX_WIDTH = 1024
N_BRANCH = 4
POOL_GROUPS = 4
POOL_GROUP_DIM = MIX_WIDTH // POOL_GROUPS
POOL_WINDOWS = (2, 4, 8, 16)
B_HEADS = 8
B_KV_HEADS = 2
GQA_GROUP = B_HEADS // B_KV_HEADS
B_WINDOW = 128
B_BLOCK = 128
C_HEADS = 8
NA_KH = 8
NA_KW = 16
SSM_HEADS = 16
SSM_HEADDIM = 64
SSM_INNER = SSM_HEADS * SSM_HEADDIM
SSM_GROUPS = 2
SSM_STATE = 64
SSM_CONV = 5
SSM_CHUNK = 128
SSM_XBC = SSM_INNER + 2 * SSM_GROUPS * SSM_STATE
D_FF = 4 * D_MODEL
Q_BLOCK = 128
ROPE_BASE = 10000.0
LN_EPS = 1e-5
RMS_EPS = 1e-6
NEG_INF = -1e30
ALPHA = (2 * DEPTH) ** 0.25

N_CTX_TOK = BATCH * SEQ
N_LAT_TOK = DEC_BATCH * DEC_SEQ
N_TOK = N_CTX_TOK + N_LAT_TOK
MOD_ROWS = DEC_SEQ
N_MOD_BLOCKS = N_TOK // MOD_ROWS
N_COND = 1 + DEC_BATCH
COND_PAD = 16

U_QB = 0
U_QC = 1024
U_KC = 2048
U_VC = 3072
U_POOL = 4096
U_XBC = 5120
U_KB = 6400
U_VB = 6656
U_DT = 6912
U_Z = 7168
U_WIDTH = 8192

VMEM_LIMIT = 56 * 1024 * 1024


def _params(sem):
    return pltpu.CompilerParams(dimension_semantics=sem, vmem_limit_bytes=VMEM_LIMIT)


def _ada_kernel(cond_ref, w_ref, b_ref, o_ref):
    cond = cond_ref[...]
    s = (cond * jax.nn.sigmoid(cond)).astype(jnp.bfloat16)
    acc = jnp.dot(s, w_ref[...].astype(jnp.bfloat16), preferred_element_type=jnp.float32)
    o_ref[...] = acc + b_ref[...]


def ada_modulation(cond, w_ada, b_ada):
    bn = 1024
    n = w_ada.shape[-1]
    return pl.pallas_call(
        _ada_kernel,
        grid=(DEPTH, n // bn),
        in_specs=[
            pl.BlockSpec((COND_PAD, D_MODEL), lambda l, j: (0, 0)),
            pl.BlockSpec((None, D_MODEL, bn), lambda l, j: (l, 0, j)),
            pl.BlockSpec((None, 1, bn), lambda l, j: (l, 0, j)),
        ],
        out_specs=pl.BlockSpec((None, COND_PAD, bn), lambda l, j: (l, 0, j)),
        out_shape=jax.ShapeDtypeStruct((DEPTH, COND_PAD, n), jnp.float32),
        compiler_params=_params(("arbitrary", "arbitrary")),
        name="ada_modulation",
    )(cond, w_ada, b_ada)


def _modulate_kernel(x_ref, scale_ref, shift_ref, o_ref):
    o_ref[...] = (x_ref[...] * (1.0 + scale_ref[...]) + shift_ref[...]).astype(o_ref.dtype)


def _mod_spec(which, bm):
    return pl.BlockSpec((None, None, 1, D_MODEL), lambda i, *_: (which, (i * bm) // MOD_ROWS, 0, 0))


def modulate(x, mod, which_scale, which_shift):
    bm = 512
    return pl.pallas_call(
        _modulate_kernel,
        grid=(N_TOK // bm,),
        in_specs=[
            pl.BlockSpec((bm, D_MODEL), lambda i: (i, 0)),
            _mod_spec(which_scale, bm),
            _mod_spec(which_shift, bm),
        ],
        out_specs=pl.BlockSpec((bm, D_MODEL), lambda i: (i, 0)),
        out_shape=jax.ShapeDtypeStruct((N_TOK, D_MODEL), jnp.bfloat16),
        compiler_params=_params(("arbitrary",)),
        name="modulate",
    )(x, mod, mod)


def _matmul_kernel(x_ref, w_ref, o_ref, *, relu_sq):
    acc = jnp.dot(x_ref[...], w_ref[...], preferred_element_type=jnp.float32)
    if relu_sq:
        acc = jnp.square(jnp.maximum(acc, 0.0))
    o_ref[...] = acc.astype(o_ref.dtype)


def matmul(x, w, out_dtype, relu_sq=False, bm=1024, bn=1024, name="matmul"):
    m, k = x.shape
    n = w.shape[1]
    return pl.pallas_call(
        functools.partial(_matmul_kernel, relu_sq=relu_sq),
        grid=(m // bm, n // bn),
        in_specs=[
            pl.BlockSpec((bm, k), lambda i, j: (i, 0)),
            pl.BlockSpec((k, bn), lambda i, j: (0, j)),
        ],
        out_specs=pl.BlockSpec((bm, bn), lambda i, j: (i, j)),
        out_shape=jax.ShapeDtypeStruct((m, n), out_dtype),
        compiler_params=_params(("arbitrary", "arbitrary")),
        name=name,
    )(x, w)


def _merge_kernel(h_ref, ya_ref, yb_ref, yc_ref, yd_ref, wg_ref, wb_ref, o_ref):
    h = h_ref[...]
    acc = None
    for n, y_ref in enumerate((ya_ref, yb_ref, yc_ref, yd_ref)):
        gate = jax.nn.sigmoid(jnp.dot(h, wg_ref[n], preferred_element_type=jnp.float32))
        proj = jnp.dot(y_ref[...].astype(jnp.bfloat16), wb_ref[n], preferred_element_type=jnp.float32)
        acc = gate * proj if acc is None else acc + gate * proj
    o_ref[...] = acc.astype(o_ref.dtype)


def merge_branches(h, ys, w_gate, w_branch):
    bm, bn = 512, 512
    y_spec = pl.BlockSpec((bm, MIX_WIDTH), lambda i, j: (i, 0))
    return pl.pallas_call(
        _merge_kernel,
        grid=(N_TOK // bm, D_MODEL // bn),
        in_specs=[
            pl.BlockSpec((bm, D_MODEL), lambda i, j: (i, 0)),
            y_spec, y_spec, y_spec, y_spec,
            pl.BlockSpec((N_BRANCH, D_MODEL, bn), lambda i, j: (0, 0, j)),
            pl.BlockSpec((N_BRANCH, MIX_WIDTH, bn), lambda i, j: (0, 0, j)),
        ],
        out_specs=pl.BlockSpec((bm, bn), lambda i, j: (i, j)),
        out_shape=jax.ShapeDtypeStruct((N_TOK, D_MODEL), jnp.bfloat16),
        compiler_params=_params(("arbitrary", "arbitrary")),
        name="merge_branches",
    )(h, *ys, w_gate, w_branch)


def _matmul_ln_kernel(a_ref, w_ref, res_ref, gate_ref, g_ref, b_ref, scale_ref, shift_ref,
                      x_out_ref, h_out_ref, acc_ref, *, nk):
    k = pl.program_id(1)
    part = jnp.dot(a_ref[...], w_ref[...], preferred_element_type=jnp.float32)

    if nk > 1:
        @pl.when(k == 0)
        def _():
            acc_ref[...] = part

        @pl.when(k > 0)
        def _():
            acc_ref[...] += part

    @pl.when(k == nk - 1)
    def _():
        total = acc_ref[...] if nk > 1 else part
        r = ALPHA * res_ref[...] + gate_ref[...] * total
        mu = jnp.mean(r, axis=-1, keepdims=True)
        d = r - mu
        var = jnp.mean(jnp.square(d), axis=-1, keepdims=True)
        xn = d * lax.rsqrt(var + LN_EPS) * g_ref[...] + b_ref[...]
        x_out_ref[...] = xn
        h_out_ref[...] = (xn * (1.0 + scale_ref[...]) + shift_ref[...]).astype(h_out_ref.dtype)


def matmul_ln(a, w, res, mod, which_gate, ln_g, ln_b, mod_next, which_scale, which_shift, name):
    m, kdim = a.shape
    bm, bk = 512, 2048
    nk = kdim // bk
    row = pl.BlockSpec((bm, D_MODEL), lambda i, k: (i, 0))
    vec = pl.BlockSpec((1, D_MODEL), lambda i, k: (0, 0))
    return pl.pallas_call(
        functools.partial(_matmul_ln_kernel, nk=nk),
        grid=(m // bm, nk),
        in_specs=[
            pl.BlockSpec((bm, bk), lambda i, k: (i, k)),
            pl.BlockSpec((bk, D_MODEL), lambda i, k: (k, 0)),
            row,
            _mod_spec(which_gate, bm),
            vec, vec,
            _mod_spec(which_scale, bm),
            _mod_spec(which_shift, bm),
        ],
        out_specs=[row, row],
        out_shape=[jax.ShapeDtypeStruct((m, D_MODEL), jnp.float32),
                   jax.ShapeDtypeStruct((m, D_MODEL), jnp.bfloat16)],
        scratch_shapes=[pltpu.VMEM((bm, D_MODEL), jnp.float32)],
        compiler_params=_params(("arbitrary", "arbitrary")),
        name=name,
    )(a, w, res, mod, ln_g, ln_b, mod_next, mod_next)


def _rope_2d(x):
    L = x.shape[1]
    t = jnp.arange(L)
    half = HD // 2
    quarter = half // 2
    freqs = ROPE_BASE ** (-jnp.arange(quarter, dtype=jnp.float32) / quarter)

    def rotate(xa, pos):
        ang = pos.astype(jnp.float32)[:, None] * freqs[None, :]
        cos = jnp.cos(ang)[None, :, None, :].astype(xa.dtype)
        sin = jnp.sin(ang)[None, :, None, :].astype(xa.dtype)
        x1, x2 = xa[..., :quarter], xa[..., quarter:]
        return jnp.concatenate([x1 * cos - x2 * sin, x1 * sin + x2 * cos], axis=-1)

    return jnp.concatenate([rotate(x[..., :half], t // GRID_W), rotate(x[..., half:], t % GRID_W)], axis=-1)


def _pool_mix(a, w_pool, pool_scale):
    bsz, L, _ = a.shape
    ag = a.reshape(bsz, L, POOL_GROUPS, POOL_GROUP_DIM)
    cs = jnp.pad(jnp.cumsum(ag.astype(jnp.float32), axis=1), ((0, 0), (1, 0), (0, 0), (0, 0)))
    pos = jnp.arange(L)
    pooled = []
    for g, w in enumerate(POOL_WINDOWS):
        lo = jnp.clip(pos - w // 2, 0, L)
        hi = jnp.clip(pos + w - w // 2, 0, L)
        csg = cs[:, :, g]
        cnt = (hi - lo).astype(jnp.float32)[None, :, None]
        pooled.append((csg[:, hi] - csg[:, lo]) / cnt)
    pooled = jnp.stack(pooled, axis=2).astype(a.dtype) - ag
    y = jnp.einsum('blgc,gcd->blgd', pooled, w_pool)
    return y.reshape(bsz, L, MIX_WIDTH) * pool_scale


def _dense_ctx_attention(q, k, v, sink):
    bsz, s_len, kvh, grp, _ = q.shape
    scale = HD ** -0.5
    s = jnp.einsum('bqhgd,bkhd->bhgqk', q, k).astype(jnp.float32) * scale
    if sink is not None:
        s_sink = jnp.broadcast_to(sink.astype(jnp.float32)[None, :, :, None, None], s.shape[:-1] + (1,))
        s = jnp.concatenate([s, s_sink], axis=-1)
    p = jax.nn.softmax(s, axis=-1)[..., :s_len].astype(v.dtype)
    o = jnp.einsum('bhgqk,bkhd->bqhgd', p, v)
    return o.reshape(bsz, s_len, kvh * grp * HD)


def _band_gqa_latent(q, k, v, ck, cv, sink):
    bsz, L = q.shape[:2]
    nb = L // B_BLOCK
    span = B_BLOCK + 2 * B_WINDOW
    kp = jnp.pad(k, ((0, 0), (B_WINDOW, B_WINDOW), (0, 0), (0, 0)))
    vp = jnp.pad(v, ((0, 0), (B_WINDOW, B_WINDOW), (0, 0), (0, 0)))
    idx = jnp.arange(nb)[:, None] * B_BLOCK + jnp.arange(span)[None, :]
    kb, vb = kp[:, idx], vp[:, idx]
    qb = q.reshape(bsz, nb, B_BLOCK, B_KV_HEADS, GQA_GROUP, HD)
    scale = HD ** -0.5
    s_loc = jnp.einsum('bnqhgd,bnkhd->bnhgqk', qb, kb).astype(jnp.float32) * scale
    kpos = idx - B_WINDOW
    qpos = jnp.arange(L).reshape(nb, B_BLOCK)
    valid = (kpos[:, None, :] >= 0) & (kpos[:, None, :] < L) & (jnp.abs(qpos[:, :, None] - kpos[:, None, :]) <= B_WINDOW)
    s_loc = jnp.where(valid[None, :, None, None, :, :], s_loc, NEG_INF)
    s_ctx = jnp.einsum('bnqhgd,bphd->bnhgqp', qb, ck).astype(jnp.float32) * scale
    s_sink = jnp.broadcast_to(sink.astype(jnp.float32).reshape(1, 1, B_KV_HEADS, GQA_GROUP, 1, 1), s_loc.shape[:-1] + (1,))
    p = jax.nn.softmax(jnp.concatenate([s_loc, s_ctx, s_sink], axis=-1), axis=-1).astype(v.dtype)
    n_ctx = ck.shape[1]
    o = jnp.einsum('bnhgqk,bnkhd->bnqhgd', p[..., :span], vb) + jnp.einsum('bnhgqp,bphd->bnqhgd', p[..., span:span + n_ctx], cv)
    return o.reshape(bsz, L, B_HEADS * HD)


def _neighbourhood_attn_latent(q, k, v, ck, cv, rpb):
    bsz, L = q.shape[:2]
    rows = L // GRID_W
    kh = min(NA_KH, rows)
    r = jnp.arange(rows)
    r0 = jnp.clip(r - kh // 2, 0, rows - kh)
    key_rows = r0[:, None] + jnp.arange(kh)[None, :]
    col = jnp.arange(GRID_W)
    c0 = jnp.clip(col - NA_KW // 2, 0, GRID_W - NA_KW)
    qg = q.reshape(bsz, rows, GRID_W, C_HEADS, HD)
    kg = k.reshape(bsz, rows, GRID_W, C_HEADS, HD)[:, key_rows]
    vg = v.reshape(bsz, rows, GRID_W, C_HEADS, HD)[:, key_rows]
    scale = HD ** -0.5
    s_loc = jnp.einsum('brqhd,brjkhd->brhqjk', qg, kg).astype(jnp.float32) * scale
    dr_i = key_rows - r[:, None] + (NA_KH - 1)
    dc_i = jnp.clip(col[None, :] - col[:, None], -(NA_KW - 1), NA_KW - 1) + (NA_KW - 1)
    bias = rpb[:, dr_i[:, None, :, None], dc_i[None, :, None, :]]
    s_loc = s_loc + jnp.moveaxis(bias, 0, 1).astype(jnp.float32)[None]
    col_ok = (col[None, :] >= c0[:, None]) & (col[None, :] < c0[:, None] + NA_KW)
    s_loc = jnp.where(col_ok[None, None, None, :, None, :], s_loc, NEG_INF).reshape(bsz, rows, C_HEADS, GRID_W, kh * GRID_W)
    s_ctx = jnp.einsum('brqhd,bphd->brhqp', qg, ck).astype(jnp.float32) * scale
    p = jax.nn.softmax(jnp.concatenate([s_loc, s_ctx], axis=-1), axis=-1).astype(v.dtype)
    p_loc = p[..., :kh * GRID_W].reshape(bsz, rows, C_HEADS, GRID_W, kh, GRID_W)
    o = jnp.einsum('brhqjk,brjkhd->brqhd', p_loc, vg) + jnp.einsum('brhqp,bphd->brqhd', p[..., kh * GRID_W:], cv)
    return o.reshape(bsz, L, C_HEADS * HD)


def _dw_conv_centred(x, w, b):
    pad = SSM_CONV // 2
    y = lax.conv_general_dilated(x, w[:, None, :], window_strides=(1,), padding=[(pad, pad)],
                                 dimension_numbers=('NWC', 'WIO', 'NWC'), feature_group_count=x.shape[-1])
    return y + b


def _ssd_scan(x, dt, a, bm, cm, h0):
    f32 = jnp.float32
    bsz, L, H, P = x.shape
    nc, Q = L // SSM_CHUNK, SSM_CHUNK
    rep = SSM_HEADS // SSM_GROUPS
    xc = x.astype(f32).reshape(bsz, nc, Q, H, P)
    bc = jnp.repeat(bm.astype(f32), rep, axis=2).reshape(bsz, nc, Q, H, SSM_STATE)
    cc = jnp.repeat(cm.astype(f32), rep, axis=2).reshape(bsz, nc, Q, H, SSM_STATE)
    dtc = dt.astype(f32).reshape(bsz, nc, Q, H)
    cum = jnp.cumsum(dtc * a, axis=2)
    lower = jnp.tril(jnp.ones((Q, Q), dtype=bool))[None, None, :, :, None]
    seg = jnp.exp(jnp.where(lower, cum[:, :, :, None, :] - cum[:, :, None, :, :], NEG_INF))
    w_intra = jnp.einsum('bcihn,bcjhn->bcijh', cc, bc) * seg * dtc[:, :, None, :, :]
    y_intra = jnp.einsum('bcijh,bcjhp->bcihp', w_intra, xc)
    w_state = jnp.exp(cum[:, :, -1:, :] - cum) * dtc
    states = jnp.einsum('bcjh,bcjhn,bcjhp->bchpn', w_state, bc, xc)
    chunk_decay = jnp.exp(cum[:, :, -1, :])

    def step(h, inp):
        st, dec = inp
        return h * dec[:, :, None, None] + st, h

    h_fin, h_prev = lax.scan(step, h0.astype(f32), (jnp.moveaxis(states, 1, 0), jnp.moveaxis(chunk_decay, 1, 0)))
    h_prev = jnp.moveaxis(h_prev, 0, 1)
    y_inter = jnp.einsum('bcihn,bchpn->bcihp', cc, h_prev) * jnp.exp(cum)[..., None]
    y = (y_intra + y_inter).reshape(bsz, L, H, P)
    return y.astype(x.dtype), h_fin.astype(h0.dtype)


def _rms_norm(x, g):
    xf = x.astype(jnp.float32)
    return (xf * lax.rsqrt(jnp.mean(jnp.square(xf), axis=-1, keepdims=True) + RMS_EPS)).astype(x.dtype) * g


def _ssd_mixer(z, xbc, dtr, conv_w, conv_b, a_log, dt_bias, d_skip, norm_g, h0_pair):
    bsz, L, _ = z.shape
    xbc = jax.nn.silu(_dw_conv_centred(xbc, conv_w, conv_b))
    xs, bm, cm = jnp.split(xbc, [SSM_INNER, SSM_INNER + SSM_GROUPS * SSM_STATE], axis=-1)
    xs = xs.reshape(bsz, L, SSM_HEADS, SSM_HEADDIM)
    bm = bm.reshape(bsz, L, SSM_GROUPS, SSM_STATE)
    cm = cm.reshape(bsz, L, SSM_GROUPS, SSM_STATE)
    dt = jax.nn.softplus(dtr.reshape(bsz, L, 2, SSM_HEADS) + dt_bias)
    a = -jnp.exp(a_log.astype(jnp.float32))
    y_f, h_f = _ssd_scan(xs, dt[:, :, 0], a[0], bm, cm, h0_pair[0])
    y_b, h_b = _ssd_scan(xs[:, ::-1], dt[:, ::-1, 1], a[1], bm[:, ::-1], cm[:, ::-1], h0_pair[1])
    y = y_f + y_b[:, ::-1] + xs * d_skip[:, None]
    y = y.reshape(bsz, L, SSM_INNER) * jax.nn.silu(z)
    return _rms_norm(y, norm_g), jnp.stack([h_f, h_b], axis=1)


def _mixers(u, bsz, L, latent, ctx_kv, h0_pair, w_pool, pool_scale, sink, rpb, conv_w, conv_b,
            a_log, dt_bias, d_skip, ssm_norm):
    u = u.astype(jnp.float32).reshape(bsz, L, U_WIDTH)
    a = u[..., U_POOL:U_POOL + MIX_WIDTH]
    qb = u[..., U_QB:U_QB + B_HEADS * HD].reshape(bsz, L, B_HEADS, HD)
    kb = u[..., U_KB:U_KB + B_KV_HEADS * HD].reshape(bsz, L, B_KV_HEADS, HD)
    vb = u[..., U_VB:U_VB + B_KV_HEADS * HD].reshape(bsz, L, B_KV_HEADS, HD)
    qc = u[..., U_QC:U_QC + C_HEADS * HD].reshape(bsz, L, C_HEADS, HD)
    kc = u[..., U_KC:U_KC + C_HEADS * HD].reshape(bsz, L, C_HEADS, HD)
    vc = u[..., U_VC:U_VC + C_HEADS * HD].reshape(bsz, L, C_HEADS, HD)
    z = u[..., U_Z:U_Z + SSM_INNER]
    xbc = u[..., U_XBC:U_XBC + SSM_XBC]
    dtr = u[..., U_DT:U_DT + 2 * SSM_HEADS]
    y_a = _pool_mix(a, w_pool, pool_scale)
    if latent:
        ck_b, cv_b, ck_c, cv_c = ctx_kv
        kb = _rope_2d(kb)
        y_b = _band_gqa_latent(_rope_2d(qb).reshape(bsz, L, B_KV_HEADS, GQA_GROUP, HD), kb, vb, ck_b, cv_b, sink)
        y_c = _neighbourhood_attn_latent(qc, kc, vc, ck_c, cv_c, rpb)
    else:
        y_b = _dense_ctx_attention(qb.reshape(bsz, L, B_KV_HEADS, GQA_GROUP, HD), kb, vb,
                                   sink.reshape(B_KV_HEADS, GQA_GROUP))
        y_c = _dense_ctx_attention(qc[:, :, :, None, :], kc, vc, None)
    y_d, h_fin = _ssd_mixer(z, xbc, dtr, conv_w, conv_b, a_log, dt_bias, d_skip, ssm_norm, h0_pair)
    ys = [y.reshape(bsz * L, MIX_WIDTH) for y in (y_a, y_b, y_c, y_d)]
    return ys, (kb, vb, kc, vc, h_fin)


def _pack_w_in(w_in_l):
    sizes = (MIX_WIDTH, B_HEADS * HD, B_KV_HEADS * HD, B_KV_HEADS * HD, C_HEADS * HD, C_HEADS * HD,
             C_HEADS * HD, SSM_INNER, SSM_XBC, 2 * SSM_HEADS, N_BRANCH * D_MODEL)
    offs = np.concatenate([[0], np.cumsum(sizes)]).tolist()
    pool, qb, kb, vb, qc, kc, vc, z, xbc, dt, gates = [w_in_l[:, offs[i]:offs[i + 1]] for i in range(len(sizes))]
    zeros = lambda n: jnp.zeros((D_MODEL, n), w_in_l.dtype)
    w_u = jnp.concatenate([qb, qc, kc, vc, pool, xbc, kb, vb, dt, zeros(U_Z - U_DT - 2 * SSM_HEADS), z], axis=1)
    w_gate = gates.reshape(D_MODEL, N_BRANCH, D_MODEL).transpose(1, 0, 2)
    return w_u.astype(jnp.bfloat16), w_gate.astype(jnp.bfloat16)


def kernel(x_prompt, x_sample, cache_b_k, cache_b_v, cache_c_k, cache_c_v, state_ssm, c, c_ctx, w_ada, b_ada, w_in, w_pool, pool_scale, attn_sink, na_rpb, conv_w, conv_b, a_log, dt_bias, d_skip, ssm_norm, w_branch, w_o, ln1_g, ln1_b, w_mlp1, w_mlp2, ln2_g, ln2_b):
    f32 = jnp.float32
    bf16 = jnp.bfloat16

    cond = jnp.concatenate([c_ctx[None, :], c, jnp.zeros((COND_PAD - N_COND, D_MODEL), f32)], axis=0)
    mod_all = ada_modulation(cond, w_ada, b_ada.reshape(DEPTH, 1, 6 * D_MODEL))
    blk_row = np.concatenate([np.zeros(N_CTX_TOK // MOD_ROWS, np.int32), 1 + np.arange(DEC_BATCH, dtype=np.int32)])
    mods = mod_all[:, blk_row].reshape(DEPTH, N_MOD_BLOCKS, 6, 1, D_MODEL).transpose(0, 2, 1, 3, 4)
    SHIFT1, SCALE1, GATE1, SHIFT2, SCALE2, GATE2 = range(6)

    x = jnp.concatenate([x_prompt.reshape(N_CTX_TOK, D_MODEL), x_sample.reshape(N_LAT_TOK, D_MODEL)], axis=0)
    h = modulate(x, mods[0], SCALE1, SHIFT1)
    h0 = jnp.zeros((BATCH, SSM_HEADS, SSM_HEADDIM, SSM_STATE), f32)

    bk, bv, ckk, cvv, hs = [], [], [], [], []
    for l in range(DEPTH):
        w_u, w_gate = _pack_w_in(w_in[l])
        u = matmul(h, w_u, f32, name="in_proj")
        lw = (w_pool[l], pool_scale[l], attn_sink[l], na_rpb[l], conv_w[l], conv_b[l], a_log[l],
              dt_bias[l], d_skip[l], ssm_norm[l])
        ys_ctx, (kb, vb, kc, vc, h_ctx) = _mixers(u[:N_CTX_TOK], BATCH, SEQ, False, None, (h0, h0), *lw)
        ctx_kv = (cache_b_k[:, l], cache_b_v[:, l], cache_c_k[:, l], cache_c_v[:, l])
        ys_lat, _ = _mixers(u[N_CTX_TOK:], DEC_BATCH, DEC_SEQ, True, ctx_kv,
                            (state_ssm[:, l, 0], state_ssm[:, l, 1]), *lw)
        bk.append(kb)
        bv.append(vb)
        ckk.append(kc)
        cvv.append(vc)
        hs.append(h_ctx)
        ys = [jnp.concatenate([yc, yl], axis=0) for yc, yl in zip(ys_ctx, ys_lat)]

        merged = merge_branches(h, ys, w_gate, w_branch[l].astype(bf16))
        x, h2 = matmul_ln(merged, w_o[l].astype(bf16), x, mods[l], GATE1, ln1_g[l][None], ln1_b[l][None],
                          mods[l], SCALE2, SHIFT2, name="out_proj_ln1")
        ff = matmul(h2, w_mlp1[l].astype(bf16), bf16, relu_sq=True, name="mlp_up")
        nxt = min(l + 1, DEPTH - 1)
        x, h = matmul_ln(ff, w_mlp2[l].astype(bf16), x, mods[l], GATE2, ln2_g[l][None], ln2_b[l][None],
                         mods[nxt], SCALE1, SHIFT1, name="mlp_down_ln2")

    y_prompt = x[:N_CTX_TOK].reshape(BATCH, SEQ, D_MODEL)
    y_sample = x[N_CTX_TOK:].reshape(DEC_BATCH, DEC_SEQ, D_MODEL)
    return (y_prompt, y_sample, jnp.stack(bk, axis=1), jnp.stack(bv, axis=1), jnp.stack(ckk, axis=1),
            jnp.stack(cvv, axis=1), jnp.stack(hs, axis=1))
```

```python
import functools

import jax
import jax.numpy as jnp
import numpy as np
from jax import lax
from jax.experimental import pallas as pl
from jax.experimental.pallas import tpu as pltpu

D_MODEL = 2048
BATCH = 32
SEQ = 256
DEPTH = 4
DEC_BATCH = 8
DEC_SEQ = 1024
PAST_LEN = 256
GRID_W = 64
HD = 128
MIX_WIDTH = 1024
N_BRANCH = 4
POOL_GROUPS = 4
POOL_GROUP_DIM = MIX_WIDTH // POOL_GROUPS
POOL_WINDOWS = (2, 4, 8, 16)
B_HEADS = 8
B_KV_HEADS = 2
GQA_GROUP = B_HEADS // B_KV_HEADS
B_WINDOW = 128
B_BLOCK = 128
C_HEADS = 8
NA_KH = 8
NA_KW = 16
SSM_HEADS = 16
SSM_HEADDIM = 64
SSM_INNER = SSM_HEADS * SSM_HEADDIM
SSM_GROUPS = 2
SSM_STATE = 64
SSM_CONV = 5
SSM_CHUNK = 128
SSM_XBC = SSM_INNER + 2 * SSM_GROUPS * SSM_STATE
D_FF = 4 * D_MODEL
ROPE_BASE = 10000.0
LN_EPS = 1e-5
RMS_EPS = 1e-6
NEG_INF = -1e30
ALPHA = (2 * DEPTH) ** 0.25
ATTN_SCALE = HD ** -0.5

N_CTX_TOK = BATCH * SEQ
N_LAT_TOK = DEC_BATCH * DEC_SEQ
N_TOK = N_CTX_TOK + N_LAT_TOK
MOD_ROWS = DEC_SEQ
N_MOD_BLOCKS = N_TOK // MOD_ROWS
N_COND = 1 + DEC_BATCH
COND_PAD = 16
LAT_BLK0 = N_CTX_TOK // DEC_SEQ
GRID_ROWS = DEC_SEQ // GRID_W
NA_SPAN = NA_KH * GRID_W
NA_OFFSETS = NA_KH

U_QB = 0
U_QC = 1024
U_KC = 2048
U_VC = 3072
U_POOL = 4096
U_XBC = 5120
U_KB = 6400
U_VB = 6656
U_DT = 6912
U_Z = 7168
U_WIDTH = 8192

SUBLANES = 8
VMEM_LIMIT = 56 * 1024 * 1024
_NT = (((1,), (1,)), ((), ()))


def _params(sem):
    return pltpu.CompilerParams(dimension_semantics=sem, vmem_limit_bytes=VMEM_LIMIT)


def _ada_kernel(cond_ref, w_ref, b_ref, o_ref):
    cond = cond_ref[...]
    s = (cond * jax.nn.sigmoid(cond)).astype(jnp.bfloat16)
    acc = jnp.dot(s, w_ref[...].astype(jnp.bfloat16), preferred_element_type=jnp.float32)
    o_ref[...] = acc + b_ref[...]


def ada_modulation(cond, w_ada, b_ada):
    bn = 1024
    n = w_ada.shape[-1]
    return pl.pallas_call(
        _ada_kernel,
        grid=(DEPTH, n // bn),
        in_specs=[
            pl.BlockSpec((COND_PAD, D_MODEL), lambda l, j: (0, 0)),
            pl.BlockSpec((None, D_MODEL, bn), lambda l, j: (l, 0, j)),
            pl.BlockSpec((None, 1, bn), lambda l, j: (l, 0, j)),
        ],
        out_specs=pl.BlockSpec((None, COND_PAD, bn), lambda l, j: (l, 0, j)),
        out_shape=jax.ShapeDtypeStruct((DEPTH, COND_PAD, n), jnp.float32),
        compiler_params=_params(("arbitrary", "arbitrary")),
        name="ada_modulation",
    )(cond, w_ada, b_ada)


def _modulate_kernel(x_ref, scale_ref, shift_ref, o_ref):
    o_ref[...] = (x_ref[...] * (1.0 + scale_ref[...]) + shift_ref[...]).astype(o_ref.dtype)


def _mod_spec(which, bm):
    return pl.BlockSpec((None, None, 1, D_MODEL), lambda i, *_: (which, (i * bm) // MOD_ROWS, 0, 0))


def modulate(x, mod, which_scale, which_shift):
    bm = 512
    return pl.pallas_call(
        _modulate_kernel,
        grid=(N_TOK // bm,),
        in_specs=[
            pl.BlockSpec((bm, D_MODEL), lambda i: (i, 0)),
            _mod_spec(which_scale, bm),
            _mod_spec(which_shift, bm),
        ],
        out_specs=pl.BlockSpec((bm, D_MODEL), lambda i: (i, 0)),
        out_shape=jax.ShapeDtypeStruct((N_TOK, D_MODEL), jnp.bfloat16),
        compiler_params=_params(("arbitrary",)),
        name="modulate",
    )(x, mod, mod)


def _matmul_kernel(x_ref, w_ref, o_ref, *, relu_sq):
    acc = jnp.dot(x_ref[...], w_ref[...], preferred_element_type=jnp.float32)
    if relu_sq:
        acc = jnp.square(jnp.maximum(acc, 0.0))
    o_ref[...] = acc.astype(o_ref.dtype)


def matmul(x, w, out_dtype, relu_sq=False, bm=1024, bn=1024, name="matmul"):
    m, k = x.shape
    n = w.shape[1]
    return pl.pallas_call(
        functools.partial(_matmul_kernel, relu_sq=relu_sq),
        grid=(m // bm, n // bn),
        in_specs=[
            pl.BlockSpec((bm, k), lambda i, j: (i, 0)),
            pl.BlockSpec((k, bn), lambda i, j: (0, j)),
        ],
        out_specs=pl.BlockSpec((bm, bn), lambda i, j: (i, j)),
        out_shape=jax.ShapeDtypeStruct((m, n), out_dtype),
        compiler_params=_params(("arbitrary", "arbitrary")),
        name=name,
    )(x, w)


def _merge_kernel(h_ref, ya_ref, yb_ref, yc_ref, yd_ref, wg_ref, wb_ref, o_ref):
    h = h_ref[...]
    acc = None
    for n, y_ref in enumerate((ya_ref, yb_ref, yc_ref, yd_ref)):
        gate = jax.nn.sigmoid(jnp.dot(h, wg_ref[n], preferred_element_type=jnp.float32))
        proj = jnp.dot(y_ref[...].astype(jnp.bfloat16), wb_ref[n], preferred_element_type=jnp.float32)
        acc = gate * proj if acc is None else acc + gate * proj
    o_ref[...] = acc.astype(o_ref.dtype)


def merge_branches(h, ys, w_gate, w_branch):
    bm, bn = 512, 512
    y_spec = pl.BlockSpec((bm, MIX_WIDTH), lambda i, j: (i, 0))
    return pl.pallas_call(
        _merge_kernel,
        grid=(N_TOK // bm, D_MODEL // bn),
        in_specs=[
            pl.BlockSpec((bm, D_MODEL), lambda i, j: (i, 0)),
            y_spec, y_spec, y_spec, y_spec,
            pl.BlockSpec((N_BRANCH, D_MODEL, bn), lambda i, j: (0, 0, j)),
            pl.BlockSpec((N_BRANCH, MIX_WIDTH, bn), lambda i, j: (0, 0, j)),
        ],
        out_specs=pl.BlockSpec((bm, bn), lambda i, j: (i, j)),
        out_shape=jax.ShapeDtypeStruct((N_TOK, D_MODEL), jnp.bfloat16),
        compiler_params=_params(("arbitrary", "arbitrary")),
        name="merge_branches",
    )(h, *ys, w_gate, w_branch)


def _matmul_ln_kernel(a_ref, w_ref, res_ref, gate_ref, g_ref, b_ref, scale_ref, shift_ref,
                      x_out_ref, h_out_ref, acc_ref, *, nk):
    k = pl.program_id(1)
    part = jnp.dot(a_ref[...], w_ref[...], preferred_element_type=jnp.float32)

    if nk > 1:
        @pl.when(k == 0)
        def _():
            acc_ref[...] = part

        @pl.when(k > 0)
        def _():
            acc_ref[...] += part

    @pl.when(k == nk - 1)
    def _():
        total = acc_ref[...] if nk > 1 else part
        r = ALPHA * res_ref[...] + gate_ref[...] * total
        mu = jnp.mean(r, axis=-1, keepdims=True)
        d = r - mu
        var = jnp.mean(jnp.square(d), axis=-1, keepdims=True)
        xn = d * lax.rsqrt(var + LN_EPS) * g_ref[...] + b_ref[...]
        x_out_ref[...] = xn
        h_out_ref[...] = (xn * (1.0 + scale_ref[...]) + shift_ref[...]).astype(h_out_ref.dtype)


def matmul_ln(a, w, res, mod, which_gate, ln_g, ln_b, mod_next, which_scale, which_shift, name):
    m, kdim = a.shape
    bm, bk = 512, 2048
    nk = kdim // bk
    row = pl.BlockSpec((bm, D_MODEL), lambda i, k: (i, 0))
    vec = pl.BlockSpec((1, D_MODEL), lambda i, k: (0, 0))
    return pl.pallas_call(
        functools.partial(_matmul_ln_kernel, nk=nk),
        grid=(m // bm, nk),
        in_specs=[
            pl.BlockSpec((bm, bk), lambda i, k: (i, k)),
            pl.BlockSpec((bk, D_MODEL), lambda i, k: (k, 0)),
            row,
            _mod_spec(which_gate, bm),
            vec, vec,
            _mod_spec(which_scale, bm),
            _mod_spec(which_shift, bm),
        ],
        out_specs=[row, row],
        out_shape=[jax.ShapeDtypeStruct((m, D_MODEL), jnp.float32),
                   jax.ShapeDtypeStruct((m, D_MODEL), jnp.bfloat16)],
        scratch_shapes=[pltpu.VMEM((bm, D_MODEL), jnp.float32)],
        compiler_params=_params(("arbitrary", "arbitrary")),
        name=name,
    )(a, w, res, mod, ln_g, ln_b, mod_next, mod_next)


def _softmax_pv(parts, extra_logit=None):
    m = None
    for s, _ in parts:
        mi = jnp.max(s, axis=-1, keepdims=True)
        m = mi if m is None else jnp.maximum(m, mi)
    if extra_logit is not None:
        m = jnp.maximum(m, extra_logit)
    es = [jnp.exp(s - m) for s, _ in parts]
    denom = None
    for e in es:
        li = jnp.sum(e, axis=-1, keepdims=True)
        denom = li if denom is None else denom + li
    if extra_logit is not None:
        denom = denom + jnp.exp(extra_logit - m)
    inv = 1.0 / denom
    out = None
    for e, (_, v) in zip(es, parts):
        o = jnp.dot((e * inv).astype(jnp.bfloat16), v, preferred_element_type=jnp.float32)
        out = o if out is None else out + o
    return out


def _qk(q, k):
    return lax.dot_general(q, k, _NT, preferred_element_type=jnp.float32) * ATTN_SCALE


def _head(ref, h):
    return ref[:, h * HD:(h + 1) * HD]


def _ctx_attn_kernel(sink_ref, qb_ref, kb_ref, vb_ref, qc_ref, kc_ref, vc_ref, yb_ref, yc_ref):
    bf16 = jnp.bfloat16
    for h in range(B_HEADS):
        kvh = h // GQA_GROUP
        q = _head(qb_ref, h).astype(bf16)
        k = _head(kb_ref, kvh).astype(bf16)
        v = _head(vb_ref, kvh).astype(bf16)
        o = _softmax_pv([(_qk(q, k), v)], extra_logit=sink_ref[h])
        yb_ref[:, h * HD:(h + 1) * HD] = o.astype(yb_ref.dtype)
    for h in range(C_HEADS):
        q = _head(qc_ref, h).astype(bf16)
        k = _head(kc_ref, h).astype(bf16)
        v = _head(vc_ref, h).astype(bf16)
        o = _softmax_pv([(_qk(q, k), v)])
        yc_ref[:, h * HD:(h + 1) * HD] = o.astype(yc_ref.dtype)


def ctx_attention(u, sink):
    wide = lambda col: pl.BlockSpec((SEQ, MIX_WIDTH), lambda i: (i, col // MIX_WIDTH))
    kv_w = B_KV_HEADS * HD
    narrow = lambda col: pl.BlockSpec((SEQ, kv_w), lambda i: (i, col // kv_w))
    out = jax.ShapeDtypeStruct((N_TOK, MIX_WIDTH), jnp.bfloat16)
    return pl.pallas_call(
        _ctx_attn_kernel,
        grid=(BATCH,),
        in_specs=[pl.BlockSpec(memory_space=pltpu.SMEM),
                  wide(U_QB), narrow(U_KB), narrow(U_VB), wide(U_QC), wide(U_KC), wide(U_VC)],
        out_specs=[pl.BlockSpec((SEQ, MIX_WIDTH), lambda i: (i, 0))] * 2,
        out_shape=[out, out],
        compiler_params=_params(("arbitrary",)),
        name="ctx_attention",
    )(sink, u, u, u, u, u, u)


def _rope_tables():
    half = HD // 2
    quarter = half // 2
    t = np.arange(DEC_SEQ)
    freqs = ROPE_BASE ** (-jnp.arange(quarter, dtype=jnp.float32) / quarter)
    pos = jnp.stack([jnp.asarray(t // GRID_W), jnp.asarray(t % GRID_W)], axis=1).astype(jnp.float32)
    ang = pos[:, :, None] * freqs[None, None, :]
    cos = jnp.cos(ang)
    sin = jnp.sin(ang)
    zero = jnp.zeros_like(sin)
    cos_t = jnp.concatenate([cos, cos], axis=2).reshape(DEC_SEQ, HD)
    sin_a = jnp.concatenate([-sin, zero], axis=2).reshape(DEC_SEQ, HD)
    sin_b = jnp.concatenate([zero, sin], axis=2).reshape(DEC_SEQ, HD)
    return cos_t, sin_a, sin_b


def _band_attn_kernel(sink_ref, qb_ref, kb_ref, vb_ref, ck_ref, cv_ref, cos_ref, sa_ref, sb_ref, y_in_ref,
                      y_ref, q_scr, k_scr, v_scr):
    del y_in_ref
    bf16 = jnp.bfloat16
    quarter = HD // 4
    cos, sa, sb = cos_ref[...], sa_ref[...], sb_ref[...]

    def rope(x):
        return x * cos + pltpu.roll(x, HD - quarter, 1) * sa + pltpu.roll(x, quarter, 1) * sb

    for h in range(B_HEADS):
        q_scr[:, h * HD:(h + 1) * HD] = rope(_head(qb_ref, h)).astype(bf16)
    pad = jnp.zeros((B_WINDOW, B_KV_HEADS * HD), bf16)
    for scr in (k_scr, v_scr):
        scr[0:B_WINDOW, :] = pad
        scr[B_WINDOW + DEC_SEQ:, :] = pad
    for kvh in range(B_KV_HEADS):
        k_scr[B_WINDOW:B_WINDOW + DEC_SEQ, kvh * HD:(kvh + 1) * HD] = rope(_head(kb_ref, kvh)).astype(bf16)
    v_scr[B_WINDOW:B_WINDOW + DEC_SEQ, :] = vb_ref[...].astype(bf16)
    ck = ck_ref[...].astype(bf16)
    cv = cv_ref[...].astype(bf16)

    span = B_BLOCK + 2 * B_WINDOW
    rows = GQA_GROUP * B_BLOCK
    i_idx = lax.broadcasted_iota(jnp.int32, (rows, span), 0) & (B_BLOCK - 1)
    c_idx = lax.broadcasted_iota(jnp.int32, (rows, span), 1)
    rel = c_idx - i_idx
    band_ok = (rel >= 0) & (rel <= 2 * B_WINDOW)
    grp = lax.broadcasted_iota(jnp.int32, (rows, 1), 0) // B_BLOCK

    def block(n, carry):
        r0 = pl.multiple_of(n * B_BLOCK, B_BLOCK)
        kpos = c_idx + (r0 - B_WINDOW)
        valid = band_ok & (kpos >= 0) & (kpos < DEC_SEQ)
        for kvh in range(B_KV_HEADS):
            heads = [kvh * GQA_GROUP + g for g in range(GQA_GROUP)]
            q = jnp.concatenate([q_scr[pl.ds(r0, B_BLOCK), h * HD:(h + 1) * HD] for h in heads], axis=0)
            ks = k_scr[pl.ds(r0, span), kvh * HD:(kvh + 1) * HD]
            vs = v_scr[pl.ds(r0, span), kvh * HD:(kvh + 1) * HD]
            s_loc = jnp.where(valid, _qk(q, ks), NEG_INF)
            s_ctx = _qk(q, ck[:, kvh * HD:(kvh + 1) * HD])
            sink = jnp.zeros((rows, 1), jnp.float32)
            for g, h in enumerate(heads):
                sink = jnp.where(grp == g, sink_ref[h], sink)
            o = _softmax_pv([(s_loc, vs), (s_ctx, cv[:, kvh * HD:(kvh + 1) * HD])], extra_logit=sink)
            for g, h in enumerate(heads):
                y_ref[pl.ds(r0, B_BLOCK), h * HD:(h + 1) * HD] = o[g * B_BLOCK:(g + 1) * B_BLOCK].astype(y_ref.dtype)
        return carry

    lax.fori_loop(0, DEC_SEQ // B_BLOCK, block, 0)


def band_attention(u, y_b, cache_k, cache_v, layer, sink, rope_tables):
    kv_w = B_KV_HEADS * HD
    ck = cache_k.reshape(DEC_BATCH, DEPTH, PAST_LEN, kv_w)
    cv = cache_v.reshape(DEC_BATCH, DEPTH, PAST_LEN, kv_w)
    cache_spec = pl.BlockSpec((None, None, PAST_LEN, kv_w), lambda b: (b, layer, 0, 0))
    table_spec = pl.BlockSpec((DEC_SEQ, HD), lambda b: (0, 0))
    return pl.pallas_call(
        _band_attn_kernel,
        grid=(DEC_BATCH,),
        in_specs=[pl.BlockSpec(memory_space=pltpu.SMEM),
                  pl.BlockSpec((DEC_SEQ, MIX_WIDTH), lambda b: (LAT_BLK0 + b, U_QB // MIX_WIDTH)),
                  pl.BlockSpec((DEC_SEQ, kv_w), lambda b: (LAT_BLK0 + b, U_KB // kv_w)),
                  pl.BlockSpec((DEC_SEQ, kv_w), lambda b: (LAT_BLK0 + b, U_VB // kv_w)),
                  cache_spec, cache_spec, table_spec, table_spec, table_spec,
                  pl.BlockSpec(memory_space=pl.ANY)],
        out_specs=pl.BlockSpec((DEC_SEQ, MIX_WIDTH), lambda b: (LAT_BLK0 + b, 0)),
        out_shape=jax.ShapeDtypeStruct((N_TOK, MIX_WIDTH), y_b.dtype),
        scratch_shapes=[pltpu.VMEM((DEC_SEQ, MIX_WIDTH), jnp.bfloat16),
                        pltpu.VMEM((DEC_SEQ + 2 * B_WINDOW, kv_w), jnp.bfloat16),
                        pltpu.VMEM((DEC_SEQ + 2 * B_WINDOW, kv_w), jnp.bfloat16)],
        input_output_aliases={9: 0},
        compiler_params=_params(("arbitrary",)),
        name="band_attention",
    )(sink, u, u, u, ck, cv, *rope_tables, y_b)


def _na_bias_table(rpb):
    col = np.arange(GRID_W)
    c0 = np.clip(col - NA_KW // 2, 0, GRID_W - NA_KW)
    col_ok = (col[None, :] >= c0[:, None]) & (col[None, :] < c0[:, None] + NA_KW)
    dc_i = np.clip(col[None, :] - col[:, None], -(NA_KW - 1), NA_KW - 1) + (NA_KW - 1)
    d = np.arange(NA_OFFSETS)
    j = np.arange(NA_KH)
    dr_i = j[None, :] - d[:, None] + (NA_KH - 1)
    bias = rpb[:, dr_i[:, None, :, None], dc_i[None, :, None, :]]
    bias = jnp.where(col_ok[None, None, :, None, :], bias.astype(jnp.float32), NEG_INF)
    return bias.reshape(C_HEADS, NA_OFFSETS, GRID_W, NA_SPAN)


def _na_attn_kernel(q_ref, k_ref, v_ref, ck_ref, cv_ref, tab_ref, y_in_ref, y_ref, k_scr, v_scr):
    del y_in_ref
    bf16 = jnp.bfloat16
    k_scr[...] = k_ref[...].astype(bf16)
    v_scr[...] = v_ref[...].astype(bf16)
    ck = ck_ref[...].astype(bf16)
    cv = cv_ref[...].astype(bf16)

    def row(r, carry):
        first = jnp.clip(r - NA_KH // 2, 0, GRID_ROWS - NA_KH)
        q0 = pl.multiple_of(r * GRID_W, GRID_W)
        k0 = pl.multiple_of(first * GRID_W, GRID_W)
        q = q_ref[pl.ds(q0, GRID_W), :].astype(bf16)
        ks = k_scr[pl.ds(k0, NA_SPAN), :]
        vs = v_scr[pl.ds(k0, NA_SPAN), :]
        s_loc = _qk(q, ks) + tab_ref[r - first]
        o = _softmax_pv([(s_loc, vs), (_qk(q, ck), cv)])
        y_ref[pl.ds(q0, GRID_W), :] = o.astype(y_ref.dtype)
        return carry

    lax.fori_loop(0, GRID_ROWS, row, 0)


def na_attention(u, y_c, cache_k, cache_v, layer, bias_table):
    ck = cache_k.reshape(DEC_BATCH, DEPTH, PAST_LEN, C_HEADS * HD)
    cv = cache_v.reshape(DEC_BATCH, DEPTH, PAST_LEN, C_HEADS * HD)
    cache_spec = pl.BlockSpec((None, None, PAST_LEN, HD), lambda b, h: (b, layer, 0, h))
    head = lambda col: pl.BlockSpec((DEC_SEQ, HD), lambda b, h: (LAT_BLK0 + b, col // HD + h))
    return pl.pallas_call(
        _na_attn_kernel,
        grid=(DEC_BATCH, C_HEADS),
        in_specs=[head(U_QC), head(U_KC), head(U_VC), cache_spec, cache_spec,
                  pl.BlockSpec((None, NA_OFFSETS, GRID_W, NA_SPAN), lambda b, h: (h, 0, 0, 0)),
                  pl.BlockSpec(memory_space=pl.ANY)],
        out_specs=pl.BlockSpec((DEC_SEQ, HD), lambda b, h: (LAT_BLK0 + b, h)),
        out_shape=jax.ShapeDtypeStruct((N_TOK, MIX_WIDTH), y_c.dtype),
        scratch_shapes=[pltpu.VMEM((DEC_SEQ, HD), jnp.bfloat16), pltpu.VMEM((DEC_SEQ, HD), jnp.bfloat16)],
        input_output_aliases={6: 0},
        compiler_params=_params(("arbitrary", "arbitrary")),
        name="na_attention",
    )(u, u, u, ck, cv, bias_table, y_c)


def _pool_kernel(a_ref, w_ref, scale_ref, *rest, seq):
    y_ref, pad_scr = rest[-2], rest[-1]
    halo = SUBLANES
    zeros = jnp.zeros((halo, MIX_WIDTH), jnp.float32)
    pad_scr[0:halo, :] = zeros
    pad_scr[halo + seq:, :] = zeros
    pad_scr[halo:halo + seq, :] = a_ref[...].astype(jnp.float32)
    t = lax.broadcasted_iota(jnp.int32, (seq, 1), 0)
    for g, w in enumerate(POOL_WINDOWS):
        cols = slice(g * POOL_GROUP_DIM, (g + 1) * POOL_GROUP_DIM)
        lo, hi = -(w // 2), w - w // 2
        total = None
        for d in range(lo, hi):
            term = pad_scr[pl.ds(halo + d, seq), cols]
            total = term if total is None else total + term
        cnt = (jnp.minimum(t + hi, seq) - jnp.maximum(t + lo, 0)).astype(jnp.float32)
        diff = total / cnt - pad_scr[halo:halo + seq, cols]
        y = jnp.dot(diff.astype(jnp.bfloat16), w_ref[g], preferred_element_type=jnp.float32)
        y_ref[:, cols] = (y * scale_ref[:, cols]).astype(y_ref.dtype)


def pool_mix(u, w_pool, pool_scale, y_a=None):
    latent = y_a is not None
    seq, n_seq, blk0 = (DEC_SEQ, DEC_BATCH, LAT_BLK0) if latent else (SEQ, BATCH, 0)
    in_specs = [pl.BlockSpec((seq, MIX_WIDTH), lambda i: (blk0 + i, U_POOL // MIX_WIDTH)),
                pl.BlockSpec((POOL_GROUPS, POOL_GROUP_DIM, POOL_GROUP_DIM), lambda i: (0, 0, 0)),
                pl.BlockSpec((1, MIX_WIDTH), lambda i: (0, 0))]
    args = [u, w_pool, pool_scale]
    aliases = {}
    if latent:
        in_specs.append(pl.BlockSpec(memory_space=pl.ANY))
        args.append(y_a)
        aliases = {3: 0}
    return pl.pallas_call(
        functools.partial(_pool_kernel, seq=seq),
        grid=(n_seq,),
        in_specs=in_specs,
        out_specs=pl.BlockSpec((seq, MIX_WIDTH), lambda i: (blk0 + i, 0)),
        out_shape=jax.ShapeDtypeStruct((N_TOK, MIX_WIDTH), jnp.bfloat16),
        scratch_shapes=[pltpu.VMEM((seq + 2 * SUBLANES, MIX_WIDTH), jnp.float32)],
        input_output_aliases=aliases,
        compiler_params=_params(("arbitrary",)),
        name="pool_mix_latent" if latent else "pool_mix_ctx",
    )(*args)


DT_LANES = 128


def _split3(x):
    bf16, f32 = jnp.bfloat16, jnp.float32
    hi = x.astype(bf16)
    r1 = x - hi.astype(f32)
    mid = r1.astype(bf16)
    lo = (r1 - mid.astype(f32)).astype(bf16)
    return hi, mid, lo


def _select_rows(sel, x):
    return sum(jnp.dot(sel, p, preferred_element_type=jnp.float32) for p in _split3(x))


def _select_cols(x, sel):
    return sum(jnp.dot(p, sel, preferred_element_type=jnp.float32) for p in _split3(x))


def _ssd_kernel(xbc_ref, z_ref, dtr_ref, convw_ref, convb_ref, dtb_ref, alog_ref, dskip_ref, g_ref, *rest,
                seq, latent):
    if latent:
        h0_ref, _, y_ref, pad_scr, xs_scr, bc_scr, dt_scr, yacc_scr, h_scr = rest
    else:
        y_ref, hfin_ref, pad_scr, xs_scr, bc_scr, dt_scr, yacc_scr, h_scr = rest
    f32, bf16 = jnp.float32, jnp.bfloat16
    Q = SSM_CHUNK
    nc = seq // Q
    halo = SUBLANES
    gw = SSM_INNER // SSM_GROUPS
    pair_w = 2 * SSM_HEADDIM

    zeros = jnp.zeros((halo, SSM_XBC), f32)
    pad_scr[0:halo, :] = zeros
    pad_scr[halo + seq:, :] = zeros
    pad_scr[halo:halo + seq, :] = xbc_ref[...].astype(f32)
    for c in range(nc):
        acc = convb_ref[...]
        for k in range(SSM_CONV):
            start = halo + c * Q + k - SSM_CONV // 2
            acc = acc + convw_ref[k:k + 1, :] * pad_scr[start:start + Q, :]
        act = acc * jax.nn.sigmoid(acc)
        xs_scr[c * Q:(c + 1) * Q, :] = act[:, :SSM_INNER]
        bc_scr[c * Q:(c + 1) * Q, :] = act[:, SSM_INNER:]
        yacc_scr[c * Q:(c + 1) * Q, :] = act[:, :SSM_INNER] * dskip_ref[...]

    pre = dtr_ref[...].astype(f32) + dtb_ref[...]
    dt_scr[...] = jnp.maximum(pre, 0.0) + jnp.log1p(jnp.exp(-jnp.abs(pre)))
    a = -jnp.exp(alog_ref[...])

    if latent:
        h_scr[...] = h0_ref[...]
    else:
        h_scr[...] = jnp.zeros(h_scr.shape, f32)

    ri = lax.broadcasted_iota(jnp.int32, (Q, Q), 0)
    ci = lax.broadcasted_iota(jnp.int32, (Q, Q), 1)
    causal = (ci <= ri, ci >= ri)
    tri = tuple(m.astype(bf16) for m in causal)
    er = lax.broadcasted_iota(jnp.int32, (DT_LANES, SSM_INNER), 0)
    ec = lax.broadcasted_iota(jnp.int32, (DT_LANES, SSM_INNER), 1) // SSM_HEADDIM
    expand = tuple((er == ec + d * SSM_HEADS).astype(bf16) for d in range(2))
    lane = lax.broadcasted_iota(jnp.int32, (Q, pair_w), 1)
    first_head = lane < SSM_HEADDIM

    def step(s, carry):
        for d in range(2):
            c = s if d == 0 else nc - 1 - s
            r0 = pl.multiple_of(c * Q, Q)
            dtc = dt_scr[pl.ds(r0, Q), :]
            cum = _select_rows(tri[d], dtc * a)
            last = cum[Q - 1:Q, :] if d == 0 else cum[0:1, :]
            w_state = jnp.exp(last - cum) * dtc
            cum_t = cum.T
            dt_t = dtc.T
            w_state_x = _select_cols(w_state, expand[d])
            ecum_x = _select_cols(jnp.exp(cum), expand[d])
            decay_x = _select_cols(jnp.broadcast_to(jnp.exp(last), (SUBLANES, DT_LANES)), expand[d])[0:1, :]
            bcv = bc_scr[pl.ds(r0, Q), :]
            bmat = bcv[:, :SSM_GROUPS * SSM_STATE]
            cmat = bcv[:, SSM_GROUPS * SSM_STATE:]
            bmat_t = bmat.T
            xs_c = xs_scr[pl.ds(r0, Q), :]
            xw = (xs_c * w_state_x).astype(bf16)
            for g in range(SSM_GROUPS):
                st = slice(g * SSM_STATE, (g + 1) * SSM_STATE)
                ch = slice(g * gw, (g + 1) * gw)
                bg = bmat[:, st].astype(bf16)
                cg = cmat[:, st].astype(bf16)
                cb = lax.dot_general(cg, bg, _NT, preferred_element_type=f32)
                h_prev = h_scr[d, :, ch]
                y_inter = jnp.dot(cg, h_prev.astype(bf16), preferred_element_type=f32) * ecum_x[:, ch]
                new_state = jnp.dot(bmat_t[st, :].astype(bf16), xw[:, ch], preferred_element_type=f32)
                h_scr[d, :, ch] = h_prev * decay_x[:, ch] + new_state
                for k in range(gw // pair_w):
                    head = (g * gw + k * pair_w) // SSM_HEADDIM
                    w_pair = []
                    for ln in (d * SSM_HEADS + head, d * SSM_HEADS + head + 1):
                        col = jnp.broadcast_to(cum[:, ln:ln + 1], (Q, Q))
                        seg = jnp.exp(jnp.where(causal[d], col - cum_t[ln:ln + 1, :], NEG_INF))
                        w_pair.append(cb * seg * dt_t[ln:ln + 1, :])
                    lhs = jnp.concatenate(w_pair, axis=1).astype(bf16)
                    pc = slice(g * gw + k * pair_w, g * gw + (k + 1) * pair_w)
                    xp = xs_c[:, pc]
                    rhs = jnp.concatenate([jnp.where(first_head, xp, 0.0), jnp.where(first_head, 0.0, xp)],
                                          axis=0).astype(bf16)
                    y_pair = jnp.dot(lhs, rhs, preferred_element_type=f32) + y_inter[:, k * pair_w:(k + 1) * pair_w]
                    yacc_scr[pl.ds(r0, Q), pc] += y_pair
        return carry

    lax.fori_loop(0, nc, step, 0)

    for c in range(nc):
        rows = slice(c * Q, (c + 1) * Q)
        zc = z_ref[rows, :].astype(f32)
        y = yacc_scr[rows, :] * (zc * jax.nn.sigmoid(zc))
        ms = jnp.mean(jnp.square(y), axis=-1, keepdims=True)
        y_ref[rows, :] = (y * lax.rsqrt(ms + RMS_EPS) * g_ref[...]).astype(y_ref.dtype)
    if not latent:
        hfin_ref[...] = h_scr[...]


def _ssd_call(u, weights, seq, n_seq, blk0, h0=None, y_d=None):
    latent = h0 is not None
    state_shape = (2, SSM_STATE, SSM_INNER)
    vec = lambda n: pl.BlockSpec((1, n), lambda i: (0, 0))
    in_specs = [pl.BlockSpec((seq, SSM_XBC), lambda i: (blk0 + i, U_XBC // SSM_XBC)),
                pl.BlockSpec((seq, SSM_INNER), lambda i: (blk0 + i, U_Z // SSM_INNER)),
                pl.BlockSpec((seq, DT_LANES), lambda i: (blk0 + i, U_DT // DT_LANES)),
                pl.BlockSpec((SUBLANES, SSM_XBC), lambda i: (0, 0)),
                vec(SSM_XBC), vec(DT_LANES), vec(DT_LANES), vec(SSM_INNER), vec(SSM_INNER)]
    args = [u, u, u, *weights]
    y_spec = pl.BlockSpec((seq, SSM_INNER), lambda i: (blk0 + i, 0))
    y_shape = jax.ShapeDtypeStruct((N_TOK, SSM_INNER), jnp.bfloat16)
    if latent:
        in_specs += [pl.BlockSpec((None,) + state_shape, lambda i: (i, 0, 0, 0)), pl.BlockSpec(memory_space=pl.ANY)]
        args += [h0, y_d]
        out_specs, out_shape, aliases = y_spec, y_shape, {len(args) - 1: 0}
    else:
        out_specs = [y_spec, pl.BlockSpec((None,) + state_shape, lambda i: (i, 0, 0, 0))]
        out_shape = [y_shape, jax.ShapeDtypeStruct((n_seq,) + state_shape, jnp.float32)]
        aliases = {}
    f32 = jnp.float32
    return pl.pallas_call(
        functools.partial(_ssd_kernel, seq=seq, latent=latent),
        grid=(n_seq,),
        in_specs=in_specs,
        out_specs=out_specs,
        out_shape=out_shape,
        scratch_shapes=[pltpu.VMEM((seq + 2 * SUBLANES, SSM_XBC), f32), pltpu.VMEM((seq, SSM_INNER), f32),
                        pltpu.VMEM((seq, 2 * SSM_GROUPS * SSM_STATE), f32), pltpu.VMEM((seq, DT_LANES), f32),
                        pltpu.VMEM((seq, SSM_INNER), f32), pltpu.VMEM(state_shape, f32)],
        input_output_aliases=aliases,
        compiler_params=_params(("arbitrary",)),
        name="ssd_latent" if latent else "ssd_ctx",
    )(*args)


def ssd_mixer(u, conv_w, conv_b, a_log, dt_bias, d_skip, norm_g, state_l):
    f32 = jnp.float32
    lane_pad = lambda v: jnp.pad(v.reshape(1, -1).astype(f32), ((0, 0), (0, DT_LANES - v.size)))
    weights = (jnp.pad(conv_w.astype(f32), ((0, SUBLANES - SSM_CONV), (0, 0))), conv_b.reshape(1, -1),
               lane_pad(dt_bias), lane_pad(a_log), jnp.repeat(d_skip, SSM_HEADDIM).reshape(1, -1),
               norm_g.reshape(1, -1))
    to_t = lambda s: s.transpose(0, 1, 4, 2, 3).reshape(s.shape[0], 2, SSM_STATE, SSM_INNER)
    y_d, h_fin_t = _ssd_call(u, weights, SEQ, BATCH, 0)
    y_d = _ssd_call(u, weights, DEC_SEQ, DEC_BATCH, LAT_BLK0, h0=to_t(state_l), y_d=y_d)
    h_fin = h_fin_t.reshape(BATCH, 2, SSM_STATE, SSM_HEADS, SSM_HEADDIM).transpose(0, 1, 3, 4, 2)
    return y_d, h_fin


def _pack_w_in(w_in_l):
    sizes = (MIX_WIDTH, B_HEADS * HD, B_KV_HEADS * HD, B_KV_HEADS * HD, C_HEADS * HD, C_HEADS * HD,
             C_HEADS * HD, SSM_INNER, SSM_XBC, 2 * SSM_HEADS, N_BRANCH * D_MODEL)
    offs = np.concatenate([[0], np.cumsum(sizes)]).tolist()
    pool, qb, kb, vb, qc, kc, vc, z, xbc, dt, gates = [w_in_l[:, offs[i]:offs[i + 1]] for i in range(len(sizes))]
    zeros = lambda n: jnp.zeros((D_MODEL, n), w_in_l.dtype)
    w_u = jnp.concatenate([qb, qc, kc, vc, pool, xbc, kb, vb, dt, zeros(U_Z - U_DT - 2 * SSM_HEADS), z], axis=1)
    w_gate = gates.reshape(D_MODEL, N_BRANCH, D_MODEL).transpose(1, 0, 2)
    return w_u.astype(jnp.bfloat16), w_gate.astype(jnp.bfloat16)


def kernel(x_prompt, x_sample, cache_b_k, cache_b_v, cache_c_k, cache_c_v, state_ssm, c, c_ctx, w_ada, b_ada, w_in, w_pool, pool_scale, attn_sink, na_rpb, conv_w, conv_b, a_log, dt_bias, d_skip, ssm_norm, w_branch, w_o, ln1_g, ln1_b, w_mlp1, w_mlp2, ln2_g, ln2_b):
    f32 = jnp.float32
    bf16 = jnp.bfloat16

    cond = jnp.concatenate([c_ctx[None, :], c, jnp.zeros((COND_PAD - N_COND, D_MODEL), f32)], axis=0)
    mod_all = ada_modulation(cond, w_ada, b_ada.reshape(DEPTH, 1, 6 * D_MODEL))
    blk_row = np.concatenate([np.zeros(N_CTX_TOK // MOD_ROWS, np.int32), 1 + np.arange(DEC_BATCH, dtype=np.int32)])
    mods = mod_all[:, blk_row].reshape(DEPTH, N_MOD_BLOCKS, 6, 1, D_MODEL).transpose(0, 2, 1, 3, 4)
    SHIFT1, SCALE1, GATE1, SHIFT2, SCALE2, GATE2 = range(6)

    x = jnp.concatenate([x_prompt.reshape(N_CTX_TOK, D_MODEL), x_sample.reshape(N_LAT_TOK, D_MODEL)], axis=0)
    h = modulate(x, mods[0], SCALE1, SHIFT1)
    rope_tables = _rope_tables()

    bk, bv, ckk, cvv, hs = [], [], [], [], []
    for l in range(DEPTH):
        w_u, w_gate = _pack_w_in(w_in[l])
        u = matmul(h, w_u, f32, name="in_proj")

        y_a = pool_mix(u, w_pool[l].astype(bf16), pool_scale[l][None])
        y_a = pool_mix(u, w_pool[l].astype(bf16), pool_scale[l][None], y_a)
        y_b, y_c = ctx_attention(u, attn_sink[l])
        y_b = band_attention(u, y_b, cache_b_k, cache_b_v, l, attn_sink[l], rope_tables)
        y_c = na_attention(u, y_c, cache_c_k, cache_c_v, l, _na_bias_table(na_rpb[l]))
        y_d, h_ctx = ssd_mixer(u, conv_w[l], conv_b[l], a_log[l], dt_bias[l], d_skip[l], ssm_norm[l],
                               state_ssm[:, l])

        u_ctx = u[:N_CTX_TOK]
        bk.append(u_ctx[:, U_KB:U_KB + B_KV_HEADS * HD].reshape(BATCH, SEQ, B_KV_HEADS, HD))
        bv.append(u_ctx[:, U_VB:U_VB + B_KV_HEADS * HD].reshape(BATCH, SEQ, B_KV_HEADS, HD))
        ckk.append(u_ctx[:, U_KC:U_KC + C_HEADS * HD].reshape(BATCH, SEQ, C_HEADS, HD))
        cvv.append(u_ctx[:, U_VC:U_VC + C_HEADS * HD].reshape(BATCH, SEQ, C_HEADS, HD))
        hs.append(h_ctx)

        merged = merge_branches(h, (y_a, y_b, y_c, y_d), w_gate, w_branch[l].astype(bf16))
        x, h2 = matmul_ln(merged, w_o[l].astype(bf16), x, mods[l], GATE1, ln1_g[l][None], ln1_b[l][None],
                          mods[l], SCALE2, SHIFT2, name="out_proj_ln1")
        ff = matmul(h2, w_mlp1[l].astype(bf16), bf16, relu_sq=True, name="mlp_up")
        nxt = min(l + 1, DEPTH - 1)
        x, h = matmul_ln(ff, w_mlp2[l].astype(bf16), x, mods[l], GATE2, ln2_g[l][None], ln2_b[l][None],
                         mods[nxt], SCALE1, SHIFT1, name="mlp_down_ln2")

    y_prompt = x[:N_CTX_TOK].reshape(BATCH, SEQ, D_MODEL)
    y_sample = x[N_CTX_TOK:].reshape(DEC_BATCH, DEC_SEQ, D_MODEL)
    return (y_prompt, y_sample, jnp.stack(bk, axis=1), jnp.stack(bv, axis=1), jnp.stack(ckk, axis=1),
            jnp.stack(cvv, axis=1), jnp.stack(hs, axis=1))
```

```python
import functools

import jax
import jax.numpy as jnp
import numpy as np
from jax import lax
from jax.experimental import pallas as pl
from jax.experimental.pallas import tpu as pltpu

D_MODEL = 2048
BATCH = 32
SEQ = 256
DEPTH = 4
DEC_BATCH = 8
DEC_SEQ = 1024
PAST_LEN = 256
GRID_W = 64
HD = 128
MIX_WIDTH = 1024
N_BRANCH = 4
POOL_GROUPS = 4
POOL_GROUP_DIM = MIX_WIDTH // POOL_GROUPS
POOL_WINDOWS = (2, 4, 8, 16)
B_HEADS = 8
B_KV_HEADS = 2
GQA_GROUP = B_HEADS // B_KV_HEADS
B_WINDOW = 128
B_BLOCK = 128
C_HEADS = 8
NA_KH = 8
NA_KW = 16
SSM_HEADS = 16
SSM_HEADDIM = 64
SSM_INNER = SSM_HEADS * SSM_HEADDIM
SSM_GROUPS = 2
SSM_STATE = 64
SSM_CONV = 5
SSM_CHUNK = 128
SSM_XBC = SSM_INNER + 2 * SSM_GROUPS * SSM_STATE
D_FF = 4 * D_MODEL
ROPE_BASE = 10000.0
LN_EPS = 1e-5
RMS_EPS = 1e-6
NEG_INF = -1e30
ALPHA = (2 * DEPTH) ** 0.25
ATTN_SCALE = HD ** -0.5

N_CTX_TOK = BATCH * SEQ
N_LAT_TOK = DEC_BATCH * DEC_SEQ
N_TOK = N_CTX_TOK + N_LAT_TOK
MOD_ROWS = DEC_SEQ
N_MOD_BLOCKS = N_TOK // MOD_ROWS
N_COND = 1 + DEC_BATCH
COND_PAD = 16
LAT_BLK0 = N_CTX_TOK // DEC_SEQ
GRID_ROWS = DEC_SEQ // GRID_W

U_QB = 0
U_QC = 1024
U_KC = 2048
U_VC = 3072
U_POOL = 4096
U_XBC = 5120
U_KB = 6400
U_VB = 6656
U_DT = 6912
U_Z = 7168
U_WIDTH = 8192

SUBLANES = 8
VMEM_LIMIT = 56 * 1024 * 1024
_NT = (((1,), (1,)), ((), ()))


def _params(sem):
    return pltpu.CompilerParams(dimension_semantics=sem, vmem_limit_bytes=VMEM_LIMIT)


def _ada_kernel(cond_ref, w_ref, b_ref, o_ref):
    cond = cond_ref[...]
    s = (cond * jax.nn.sigmoid(cond)).astype(jnp.bfloat16)
    acc = jnp.dot(s, w_ref[...].astype(jnp.bfloat16), preferred_element_type=jnp.float32)
    o_ref[...] = acc + b_ref[...]


def ada_modulation(cond, w_ada, b_ada):
    bn = 1024
    n = w_ada.shape[-1]
    return pl.pallas_call(
        _ada_kernel,
        grid=(DEPTH, n // bn),
        in_specs=[
            pl.BlockSpec((COND_PAD, D_MODEL), lambda l, j: (0, 0)),
            pl.BlockSpec((None, D_MODEL, bn), lambda l, j: (l, 0, j)),
            pl.BlockSpec((None, 1, bn), lambda l, j: (l, 0, j)),
        ],
        out_specs=pl.BlockSpec((None, COND_PAD, bn), lambda l, j: (l, 0, j)),
        out_shape=jax.ShapeDtypeStruct((DEPTH, COND_PAD, n), jnp.float32),
        compiler_params=_params(("arbitrary", "arbitrary")),
        name="ada_modulation",
    )(cond, w_ada, b_ada)


def _modulate_kernel(x_ref, scale_ref, shift_ref, o_ref):
    o_ref[...] = (x_ref[...] * (1.0 + scale_ref[...]) + shift_ref[...]).astype(o_ref.dtype)


def _mod_spec(which, bm):
    return pl.BlockSpec((None, None, 1, D_MODEL), lambda i, *_: (which, (i * bm) // MOD_ROWS, 0, 0))


def modulate(x, mod, which_scale, which_shift):
    bm = 512
    return pl.pallas_call(
        _modulate_kernel,
        grid=(N_TOK // bm,),
        in_specs=[
            pl.BlockSpec((bm, D_MODEL), lambda i: (i, 0)),
            _mod_spec(which_scale, bm),
            _mod_spec(which_shift, bm),
        ],
        out_specs=pl.BlockSpec((bm, D_MODEL), lambda i: (i, 0)),
        out_shape=jax.ShapeDtypeStruct((N_TOK, D_MODEL), jnp.bfloat16),
        compiler_params=_params(("arbitrary",)),
        name="modulate",
    )(x, mod, mod)


def _matmul_kernel(x_ref, w_ref, o_ref, *, relu_sq):
    acc = jnp.dot(x_ref[...], w_ref[...], preferred_element_type=jnp.float32)
    if relu_sq:
        acc = jnp.square(jnp.maximum(acc, 0.0))
    o_ref[...] = acc.astype(o_ref.dtype)


def matmul(x, w, out_dtype, relu_sq=False, bm=1024, bn=1024, name="matmul"):
    m, k = x.shape
    n = w.shape[1]
    return pl.pallas_call(
        functools.partial(_matmul_kernel, relu_sq=relu_sq),
        grid=(m // bm, n // bn),
        in_specs=[
            pl.BlockSpec((bm, k), lambda i, j: (i, 0)),
            pl.BlockSpec((k, bn), lambda i, j: (0, j)),
        ],
        out_specs=pl.BlockSpec((bm, bn), lambda i, j: (i, j)),
        out_shape=jax.ShapeDtypeStruct((m, n), out_dtype),
        compiler_params=_params(("arbitrary", "arbitrary")),
        name=name,
    )(x, w)


def _merge_kernel(h_ref, ya_ref, yb_ref, yc_ref, yd_ref, wg_ref, wb_ref, o_ref):
    h = h_ref[...]
    acc = None
    for n, y_ref in enumerate((ya_ref, yb_ref, yc_ref, yd_ref)):
        gate = jax.nn.sigmoid(jnp.dot(h, wg_ref[n], preferred_element_type=jnp.float32))
        proj = jnp.dot(y_ref[...].astype(jnp.bfloat16), wb_ref[n], preferred_element_type=jnp.float32)
        acc = gate * proj if acc is None else acc + gate * proj
    o_ref[...] = acc.astype(o_ref.dtype)


def merge_branches(h, ys, w_gate, w_branch):
    bm, bn = 512, 512
    y_spec = pl.BlockSpec((bm, MIX_WIDTH), lambda i, j: (i, 0))
    return pl.pallas_call(
        _merge_kernel,
        grid=(N_TOK // bm, D_MODEL // bn),
        in_specs=[
            pl.BlockSpec((bm, D_MODEL), lambda i, j: (i, 0)),
            y_spec, y_spec, y_spec, y_spec,
            pl.BlockSpec((N_BRANCH, D_MODEL, bn), lambda i, j: (0, 0, j)),
            pl.BlockSpec((N_BRANCH, MIX_WIDTH, bn), lambda i, j: (0, 0, j)),
        ],
        out_specs=pl.BlockSpec((bm, bn), lambda i, j: (i, j)),
        out_shape=jax.ShapeDtypeStruct((N_TOK, D_MODEL), jnp.bfloat16),
        compiler_params=_params(("arbitrary", "arbitrary")),
        name="merge_branches",
    )(h, *ys, w_gate, w_branch)


def _matmul_ln_kernel(a_ref, w_ref, res_ref, gate_ref, g_ref, b_ref, scale_ref, shift_ref,
                      x_out_ref, h_out_ref, acc_ref, *, nk):
    k = pl.program_id(1)
    part = jnp.dot(a_ref[...], w_ref[...], preferred_element_type=jnp.float32)

    if nk > 1:
        @pl.when(k == 0)
        def _():
            acc_ref[...] = part

        @pl.when(k > 0)
        def _():
            acc_ref[...] += part

    @pl.when(k == nk - 1)
    def _():
        total = acc_ref[...] if nk > 1 else part
        r = ALPHA * res_ref[...] + gate_ref[...] * total
        mu = jnp.mean(r, axis=-1, keepdims=True)
        d = r - mu
        var = jnp.mean(jnp.square(d), axis=-1, keepdims=True)
        xn = d * lax.rsqrt(var + LN_EPS) * g_ref[...] + b_ref[...]
        x_out_ref[...] = xn
        h_out_ref[...] = (xn * (1.0 + scale_ref[...]) + shift_ref[...]).astype(h_out_ref.dtype)


def matmul_ln(a, w, res, mod, which_gate, ln_g, ln_b, mod_next, which_scale, which_shift, name):
    m, kdim = a.shape
    bm, bk = 512, 2048
    nk = kdim // bk
    row = pl.BlockSpec((bm, D_MODEL), lambda i, k: (i, 0))
    vec = pl.BlockSpec((1, D_MODEL), lambda i, k: (0, 0))
    return pl.pallas_call(
        functools.partial(_matmul_ln_kernel, nk=nk),
        grid=(m // bm, nk),
        in_specs=[
            pl.BlockSpec((bm, bk), lambda i, k: (i, k)),
            pl.BlockSpec((bk, D_MODEL), lambda i, k: (k, 0)),
            row,
            _mod_spec(which_gate, bm),
            vec, vec,
            _mod_spec(which_scale, bm),
            _mod_spec(which_shift, bm),
        ],
        out_specs=[row, row],
        out_shape=[jax.ShapeDtypeStruct((m, D_MODEL), jnp.float32),
                   jax.ShapeDtypeStruct((m, D_MODEL), jnp.bfloat16)],
        scratch_shapes=[pltpu.VMEM((bm, D_MODEL), jnp.float32)],
        compiler_params=_params(("arbitrary", "arbitrary")),
        name=name,
    )(a, w, res, mod, ln_g, ln_b, mod_next, mod_next)


def _softmax_pv(parts, extra_logit=None):
    m = None
    for s, _ in parts:
        mi = jnp.max(s, axis=-1, keepdims=True)
        m = mi if m is None else jnp.maximum(m, mi)
    if extra_logit is not None:
        m = jnp.maximum(m, extra_logit)
    es = [jnp.exp(s - m) for s, _ in parts]
    denom = None
    for e in es:
        li = jnp.sum(e, axis=-1, keepdims=True)
        denom = li if denom is None else denom + li
    if extra_logit is not None:
        denom = denom + jnp.exp(extra_logit - m)
    inv = 1.0 / denom
    out = None
    for e, (_, v) in zip(es, parts):
        o = jnp.dot((e * inv).astype(jnp.bfloat16), v, preferred_element_type=jnp.float32)
        out = o if out is None else out + o
    return out


def _qk(q, k):
    return lax.dot_general(q, k, _NT, preferred_element_type=jnp.float32) * ATTN_SCALE


def _head(ref, h):
    return ref[:, h * HD:(h + 1) * HD]


def _ctx_attn_kernel(sink_ref, qb_ref, kb_ref, vb_ref, qc_ref, kc_ref, vc_ref, yb_ref, yc_ref):
    bf16 = jnp.bfloat16
    for h in range(B_HEADS):
        kvh = h // GQA_GROUP
        q = _head(qb_ref, h).astype(bf16)
        k = _head(kb_ref, kvh).astype(bf16)
        v = _head(vb_ref, kvh).astype(bf16)
        o = _softmax_pv([(_qk(q, k), v)], extra_logit=sink_ref[h])
        yb_ref[:, h * HD:(h + 1) * HD] = o.astype(yb_ref.dtype)
    for h in range(C_HEADS):
        q = _head(qc_ref, h).astype(bf16)
        k = _head(kc_ref, h).astype(bf16)
        v = _head(vc_ref, h).astype(bf16)
        o = _softmax_pv([(_qk(q, k), v)])
        yc_ref[:, h * HD:(h + 1) * HD] = o.astype(yc_ref.dtype)


def ctx_attention(u, sink):
    wide = lambda col: pl.BlockSpec((SEQ, MIX_WIDTH), lambda i: (i, col // MIX_WIDTH))
    kv_w = B_KV_HEADS * HD
    narrow = lambda col: pl.BlockSpec((SEQ, kv_w), lambda i: (i, col // kv_w))
    out = jax.ShapeDtypeStruct((N_TOK, MIX_WIDTH), jnp.bfloat16)
    return pl.pallas_call(
        _ctx_attn_kernel,
        grid=(BATCH,),
        in_specs=[pl.BlockSpec(memory_space=pltpu.SMEM),
                  wide(U_QB), narrow(U_KB), narrow(U_VB), wide(U_QC), wide(U_KC), wide(U_VC)],
        out_specs=[pl.BlockSpec((SEQ, MIX_WIDTH), lambda i: (i, 0))] * 2,
        out_shape=[out, out],
        compiler_params=_params(("arbitrary",)),
        name="ctx_attention",
    )(sink, u, u, u, u, u, u)


def _rope_tables():
    half = HD // 2
    quarter = half // 2
    t = np.arange(DEC_SEQ)
    freqs = ROPE_BASE ** (-jnp.arange(quarter, dtype=jnp.float32) / quarter)
    pos = jnp.stack([jnp.asarray(t // GRID_W), jnp.asarray(t % GRID_W)], axis=1).astype(jnp.float32)
    ang = pos[:, :, None] * freqs[None, None, :]
    cos = jnp.cos(ang)
    sin = jnp.sin(ang)
    zero = jnp.zeros_like(sin)
    cos_t = jnp.concatenate([cos, cos], axis=2).reshape(DEC_SEQ, HD)
    sin_a = jnp.concatenate([-sin, zero], axis=2).reshape(DEC_SEQ, HD)
    sin_b = jnp.concatenate([zero, sin], axis=2).reshape(DEC_SEQ, HD)
    return cos_t, sin_a, sin_b


def _band_attn_kernel(sink_ref, qb_ref, kb_ref, vb_ref, ck_ref, cv_ref, cos_ref, sa_ref, sb_ref, y_in_ref,
                      y_ref, q_scr, k_scr, v_scr):
    del y_in_ref
    bf16 = jnp.bfloat16
    quarter = HD // 4
    cos, sa, sb = cos_ref[...], sa_ref[...], sb_ref[...]

    def rope(x):
        return x * cos + pltpu.roll(x, HD - quarter, 1) * sa + pltpu.roll(x, quarter, 1) * sb

    for h in range(B_HEADS):
        q_scr[:, h * HD:(h + 1) * HD] = rope(_head(qb_ref, h).astype(jnp.float32)).astype(bf16)
    pad = jnp.zeros((B_WINDOW, B_KV_HEADS * HD), bf16)
    for scr in (k_scr, v_scr):
        scr[0:B_WINDOW, :] = pad
        scr[B_WINDOW + DEC_SEQ:, :] = pad
    for kvh in range(B_KV_HEADS):
        k_scr[B_WINDOW:B_WINDOW + DEC_SEQ, kvh * HD:(kvh + 1) * HD] = rope(_head(kb_ref, kvh).astype(jnp.float32)).astype(bf16)
    v_scr[B_WINDOW:B_WINDOW + DEC_SEQ, :] = vb_ref[...].astype(bf16)
    ck = ck_ref[...].astype(bf16)
    cv = cv_ref[...].astype(bf16)

    span = B_BLOCK + 2 * B_WINDOW
    rows = GQA_GROUP * B_BLOCK
    i_idx = lax.broadcasted_iota(jnp.int32, (rows, span), 0) & (B_BLOCK - 1)
    c_idx = lax.broadcasted_iota(jnp.int32, (rows, span), 1)
    rel = c_idx - i_idx
    band_ok = (rel >= 0) & (rel <= 2 * B_WINDOW)
    grp = lax.broadcasted_iota(jnp.int32, (rows, 1), 0) // B_BLOCK

    def block(n, carry):
        r0 = pl.multiple_of(n * B_BLOCK, B_BLOCK)
        kpos = c_idx + (r0 - B_WINDOW)
        valid = band_ok & (kpos >= 0) & (kpos < DEC_SEQ)
        for kvh in range(B_KV_HEADS):
            heads = [kvh * GQA_GROUP + g for g in range(GQA_GROUP)]
            q = jnp.concatenate([q_scr[pl.ds(r0, B_BLOCK), h * HD:(h + 1) * HD] for h in heads], axis=0)
            ks = k_scr[pl.ds(r0, span), kvh * HD:(kvh + 1) * HD]
            vs = v_scr[pl.ds(r0, span), kvh * HD:(kvh + 1) * HD]
            s_loc = jnp.where(valid, _qk(q, ks), NEG_INF)
            s_ctx = _qk(q, ck[:, kvh * HD:(kvh + 1) * HD])
            sink = jnp.zeros((rows, 1), jnp.float32)
            for g, h in enumerate(heads):
                sink = jnp.where(grp == g, sink_ref[h], sink)
            o = _softmax_pv([(s_loc, vs), (s_ctx, cv[:, kvh * HD:(kvh + 1) * HD])], extra_logit=sink)
            for g, h in enumerate(heads):
                y_ref[pl.ds(r0, B_BLOCK), h * HD:(h + 1) * HD] = o[g * B_BLOCK:(g + 1) * B_BLOCK].astype(y_ref.dtype)
        return carry

    lax.fori_loop(0, DEC_SEQ // B_BLOCK, block, 0)


def band_attention(u, y_b, cache_k, cache_v, layer, sink, rope_tables):
    kv_w = B_KV_HEADS * HD
    ck = cache_k.reshape(DEC_BATCH, DEPTH, PAST_LEN, kv_w)
    cv = cache_v.reshape(DEC_BATCH, DEPTH, PAST_LEN, kv_w)
    cache_spec = pl.BlockSpec((None, None, PAST_LEN, kv_w), lambda b: (b, layer, 0, 0))
    table_spec = pl.BlockSpec((DEC_SEQ, HD), lambda b: (0, 0))
    return pl.pallas_call(
        _band_attn_kernel,
        grid=(DEC_BATCH,),
        in_specs=[pl.BlockSpec(memory_space=pltpu.SMEM),
                  pl.BlockSpec((DEC_SEQ, MIX_WIDTH), lambda b: (LAT_BLK0 + b, U_QB // MIX_WIDTH)),
                  pl.BlockSpec((DEC_SEQ, kv_w), lambda b: (LAT_BLK0 + b, U_KB // kv_w)),
                  pl.BlockSpec((DEC_SEQ, kv_w), lambda b: (LAT_BLK0 + b, U_VB // kv_w)),
                  cache_spec, cache_spec, table_spec, table_spec, table_spec,
                  pl.BlockSpec(memory_space=pl.ANY)],
        out_specs=pl.BlockSpec((DEC_SEQ, MIX_WIDTH), lambda b: (LAT_BLK0 + b, 0)),
        out_shape=jax.ShapeDtypeStruct((N_TOK, MIX_WIDTH), y_b.dtype),
        scratch_shapes=[pltpu.VMEM((DEC_SEQ, MIX_WIDTH), jnp.bfloat16),
                        pltpu.VMEM((DEC_SEQ + 2 * B_WINDOW, kv_w), jnp.bfloat16),
                        pltpu.VMEM((DEC_SEQ + 2 * B_WINDOW, kv_w), jnp.bfloat16)],
        input_output_aliases={9: 0},
        compiler_params=_params(("arbitrary",)),
        name="band_attention",
    )(sink, u, u, u, ck, cv, *rope_tables, y_b)


def _na_bias_tables(na_rpb):
    col = np.arange(GRID_W)
    c0 = np.clip(col - NA_KW // 2, 0, GRID_W - NA_KW)
    col_ok = (col[None, :] >= c0[:, None]) & (col[None, :] < c0[:, None] + NA_KW)
    dc_i = np.clip(col[None, :] - col[:, None], -(NA_KW - 1), NA_KW - 1) + (NA_KW - 1)
    onehot = (np.arange(2 * NA_KW - 1)[:, None, None] == dc_i[None]).astype(np.float32)
    toeplitz = jnp.einsum('lhrc,cqk->lhrqk', na_rpb.astype(jnp.float32), onehot,
                          precision=lax.Precision.HIGHEST)
    toeplitz = jnp.where(col_ok, toeplitz, NEG_INF)
    masked = jnp.full((DEPTH, C_HEADS, GRID_W, GRID_W), NEG_INF, jnp.float32)
    row_blocks = []
    for rq in range(GRID_ROWS):
        first = _na_first_key_row(rq)
        pieces = [toeplitz[:, :, rk - rq + NA_KH - 1] if first <= rk < first + NA_KH else masked
                  for rk in range(GRID_ROWS)]
        row_blocks.append(jnp.concatenate(pieces, axis=-1))
    return jnp.concatenate(row_blocks, axis=-2)


def _na_first_key_row(rq):
    return min(max(rq - NA_KH // 2, 0), GRID_ROWS - NA_KH)


NA_Q_ROWS = 4


def _na_attn_kernel(q_ref, k_ref, v_ref, ck_ref, cv_ref, tab_ref, y_in_ref, y_ref):
    del y_in_ref
    bf16 = jnp.bfloat16
    ck = ck_ref[...].astype(bf16)
    cv = cv_ref[...].astype(bf16)
    for rq0 in range(0, GRID_ROWS, NA_Q_ROWS):
        lo = _na_first_key_row(rq0) * GRID_W // HD * HD
        hi = -(-(_na_first_key_row(rq0 + NA_Q_ROWS - 1) + NA_KH) * GRID_W // HD) * HD
        rows = slice(rq0 * GRID_W, (rq0 + NA_Q_ROWS) * GRID_W)
        q = q_ref[rows, :]
        s_loc = _qk(q, k_ref[lo:hi, :]) + tab_ref[rows, lo:hi]
        o = _softmax_pv([(s_loc, v_ref[lo:hi, :]), (_qk(q, ck), cv)])
        y_ref[rows, :] = o.astype(y_ref.dtype)


def na_attention(u, y_c, cache_k, cache_v, layer, bias_tables):
    assert u.dtype == jnp.bfloat16
    ck = cache_k.reshape(DEC_BATCH, DEPTH, PAST_LEN, C_HEADS * HD)
    cv = cache_v.reshape(DEC_BATCH, DEPTH, PAST_LEN, C_HEADS * HD)
    cache_spec = pl.BlockSpec((None, None, PAST_LEN, HD), lambda h, b: (b, layer, 0, h))
    head = lambda col: pl.BlockSpec((DEC_SEQ, HD), lambda h, b: (LAT_BLK0 + b, col // HD + h))
    return pl.pallas_call(
        _na_attn_kernel,
        grid=(C_HEADS, DEC_BATCH),
        in_specs=[head(U_QC), head(U_KC), head(U_VC), cache_spec, cache_spec,
                  pl.BlockSpec((None, None, DEC_SEQ, DEC_SEQ), lambda h, b: (layer, h, 0, 0)),
                  pl.BlockSpec(memory_space=pl.ANY)],
        out_specs=pl.BlockSpec((DEC_SEQ, HD), lambda h, b: (LAT_BLK0 + b, h)),
        out_shape=jax.ShapeDtypeStruct((N_TOK, MIX_WIDTH), y_c.dtype),
        input_output_aliases={6: 0},
        compiler_params=_params(("arbitrary", "arbitrary")),
        name="na_attention",
    )(u, u, u, ck, cv, bias_tables, y_c)


def _pool_kernel(a_ref, w_ref, scale_ref, *rest, seq):
    y_ref, pad_scr = rest[-2], rest[-1]
    halo = SUBLANES
    zeros = jnp.zeros((halo, MIX_WIDTH), jnp.float32)
    pad_scr[0:halo, :] = zeros
    pad_scr[halo + seq:, :] = zeros
    pad_scr[halo:halo + seq, :] = a_ref[...].astype(jnp.float32)
    t = lax.broadcasted_iota(jnp.int32, (seq, 1), 0)
    for g, w in enumerate(POOL_WINDOWS):
        cols = slice(g * POOL_GROUP_DIM, (g + 1) * POOL_GROUP_DIM)
        lo, hi = -(w // 2), w - w // 2
        total = None
        for d in range(lo, hi):
            term = pad_scr[pl.ds(halo + d, seq), cols]
            total = term if total is None else total + term
        cnt = (jnp.minimum(t + hi, seq) - jnp.maximum(t + lo, 0)).astype(jnp.float32)
        diff = total / cnt - pad_scr[halo:halo + seq, cols]
        y = jnp.dot(diff.astype(jnp.bfloat16), w_ref[g], preferred_element_type=jnp.float32)
        y_ref[:, cols] = (y * scale_ref[:, cols]).astype(y_ref.dtype)


def pool_mix(u, w_pool, pool_scale, y_a=None):
    latent = y_a is not None
    seq, n_seq, blk0 = (DEC_SEQ, DEC_BATCH, LAT_BLK0) if latent else (SEQ, BATCH, 0)
    in_specs = [pl.BlockSpec((seq, MIX_WIDTH), lambda i: (blk0 + i, U_POOL // MIX_WIDTH)),
                pl.BlockSpec((POOL_GROUPS, POOL_GROUP_DIM, POOL_GROUP_DIM), lambda i: (0, 0, 0)),
                pl.BlockSpec((1, MIX_WIDTH), lambda i: (0, 0))]
    args = [u, w_pool, pool_scale]
    aliases = {}
    if latent:
        in_specs.append(pl.BlockSpec(memory_space=pl.ANY))
        args.append(y_a)
        aliases = {3: 0}
    return pl.pallas_call(
        functools.partial(_pool_kernel, seq=seq),
        grid=(n_seq,),
        in_specs=in_specs,
        out_specs=pl.BlockSpec((seq, MIX_WIDTH), lambda i: (blk0 + i, 0)),
        out_shape=jax.ShapeDtypeStruct((N_TOK, MIX_WIDTH), jnp.bfloat16),
        scratch_shapes=[pltpu.VMEM((seq + 2 * SUBLANES, MIX_WIDTH), jnp.float32)],
        input_output_aliases=aliases,
        compiler_params=_params(("arbitrary",)),
        name="pool_mix_latent" if latent else "pool_mix_ctx",
    )(*args)


DT_LANES = 128


def _split3(x):
    bf16, f32 = jnp.bfloat16, jnp.float32
    hi = x.astype(bf16)
    r1 = x - hi.astype(f32)
    mid = r1.astype(bf16)
    lo = (r1 - mid.astype(f32)).astype(bf16)
    return hi, mid, lo


def _select_rows(sel, x):
    return sum(jnp.dot(sel, p, preferred_element_type=jnp.float32) for p in _split3(x))


def _select_cols(x, sel):
    return sum(jnp.dot(p, sel, preferred_element_type=jnp.float32) for p in _split3(x))


def _ssd_kernel(xbc_ref, z_ref, dtr_ref, convw_ref, convb_ref, dtb_ref, alog_ref, dskip_ref, g_ref, *rest,
                seq, latent):
    if latent:
        h0_ref, _, y_ref, pad_scr, xs_scr, bc_scr, dt_scr, yacc_scr, h_scr = rest
    else:
        y_ref, hfin_ref, pad_scr, xs_scr, bc_scr, dt_scr, yacc_scr, h_scr = rest
    f32, bf16 = jnp.float32, jnp.bfloat16
    Q = SSM_CHUNK
    nc = seq // Q
    halo = SUBLANES
    gw = SSM_INNER // SSM_GROUPS
    pair_w = 2 * SSM_HEADDIM

    zeros = jnp.zeros((halo, SSM_XBC), f32)
    pad_scr[0:halo, :] = zeros
    pad_scr[halo + seq:, :] = zeros
    pad_scr[halo:halo + seq, :] = xbc_ref[...].astype(f32)
    for c in range(nc):
        acc = convb_ref[...]
        for k in range(SSM_CONV):
            start = halo + c * Q + k - SSM_CONV // 2
            acc = acc + convw_ref[k:k + 1, :] * pad_scr[start:start + Q, :]
        act = acc * jax.nn.sigmoid(acc)
        xs_scr[c * Q:(c + 1) * Q, :] = act[:, :SSM_INNER]
        bc_scr[c * Q:(c + 1) * Q, :] = act[:, SSM_INNER:]
        yacc_scr[c * Q:(c + 1) * Q, :] = act[:, :SSM_INNER] * dskip_ref[...]

    pre = dtr_ref[...].astype(f32) + dtb_ref[...]
    dt_scr[...] = jnp.maximum(pre, 0.0) + jnp.log1p(jnp.exp(-jnp.abs(pre)))
    a = -jnp.exp(alog_ref[...])

    if latent:
        h_scr[...] = h0_ref[...]
    else:
        h_scr[...] = jnp.zeros(h_scr.shape, f32)

    ri = lax.broadcasted_iota(jnp.int32, (Q, Q), 0)
    ci = lax.broadcasted_iota(jnp.int32, (Q, Q), 1)
    causal = (ci <= ri, ci >= ri)
    tri = tuple(m.astype(bf16) for m in causal)
    er = lax.broadcasted_iota(jnp.int32, (DT_LANES, SSM_INNER), 0)
    ec = lax.broadcasted_iota(jnp.int32, (DT_LANES, SSM_INNER), 1) // SSM_HEADDIM
    expand = tuple((er == ec + d * SSM_HEADS).astype(bf16) for d in range(2))
    lane = lax.broadcasted_iota(jnp.int32, (Q, pair_w), 1)
    first_head = lane < SSM_HEADDIM

    def step(s, carry):
        for d in range(2):
            c = s if d == 0 else nc - 1 - s
            r0 = pl.multiple_of(c * Q, Q)
            dtc = dt_scr[pl.ds(r0, Q), :]
            cum = _select_rows(tri[d], dtc * a)
            last = cum[Q - 1:Q, :] if d == 0 else cum[0:1, :]
            w_state = jnp.exp(last - cum) * dtc
            cum_t = cum.T
            dt_t = dtc.T
            w_state_x = _select_cols(w_state, expand[d])
            ecum_x = _select_cols(jnp.exp(cum), expand[d])
            decay_x = _select_cols(jnp.broadcast_to(jnp.exp(last), (SUBLANES, DT_LANES)), expand[d])[0:1, :]
            bcv = bc_scr[pl.ds(r0, Q), :]
            bmat = bcv[:, :SSM_GROUPS * SSM_STATE]
            cmat = bcv[:, SSM_GROUPS * SSM_STATE:]
            bmat_t = bmat.T
            xs_c = xs_scr[pl.ds(r0, Q), :]
            xw = (xs_c * w_state_x).astype(bf16)
            for g in range(SSM_GROUPS):
                st = slice(g * SSM_STATE, (g + 1) * SSM_STATE)
                ch = slice(g * gw, (g + 1) * gw)
                bg = bmat[:, st].astype(bf16)
                cg = cmat[:, st].astype(bf16)
                cb = lax.dot_general(cg, bg, _NT, preferred_element_type=f32)
                h_prev = h_scr[d, :, ch]
                y_inter = jnp.dot(cg, h_prev.astype(bf16), preferred_element_type=f32) * ecum_x[:, ch]
                new_state = jnp.dot(bmat_t[st, :].astype(bf16), xw[:, ch], preferred_element_type=f32)
                h_scr[d, :, ch] = h_prev * decay_x[:, ch] + new_state
                for k in range(gw // pair_w):
                    head = (g * gw + k * pair_w) // SSM_HEADDIM
                    w_pair = []
                    for ln in (d * SSM_HEADS + head, d * SSM_HEADS + head + 1):
                        col = jnp.broadcast_to(cum[:, ln:ln + 1], (Q, Q))
                        seg = jnp.exp(jnp.where(causal[d], col - cum_t[ln:ln + 1, :], NEG_INF))
                        w_pair.append(cb * seg * dt_t[ln:ln + 1, :])
                    lhs = jnp.concatenate(w_pair, axis=1).astype(bf16)
                    pc = slice(g * gw + k * pair_w, g * gw + (k + 1) * pair_w)
                    xp = xs_c[:, pc]
                    rhs = jnp.concatenate([jnp.where(first_head, xp, 0.0), jnp.where(first_head, 0.0, xp)],
                                          axis=0).astype(bf16)
                    y_pair = jnp.dot(lhs, rhs, preferred_element_type=f32) + y_inter[:, k * pair_w:(k + 1) * pair_w]
                    yacc_scr[pl.ds(r0, Q), pc] += y_pair
        return carry

    lax.fori_loop(0, nc, step, 0)

    for c in range(nc):
        rows = slice(c * Q, (c + 1) * Q)
        zc = z_ref[rows, :].astype(f32)
        y = yacc_scr[rows, :] * (zc * jax.nn.sigmoid(zc))
        ms = jnp.mean(jnp.square(y), axis=-1, keepdims=True)
        y_ref[rows, :] = (y * lax.rsqrt(ms + RMS_EPS) * g_ref[...]).astype(y_ref.dtype)
    if not latent:
        hfin_ref[...] = h_scr[...]


def _ssd_call(u, weights, seq, n_seq, blk0, h0=None, y_d=None):
    latent = h0 is not None
    state_shape = (2, SSM_STATE, SSM_INNER)
    vec = lambda n: pl.BlockSpec((1, n), lambda i: (0, 0))
    in_specs = [pl.BlockSpec((seq, SSM_XBC), lambda i: (blk0 + i, U_XBC // SSM_XBC)),
                pl.BlockSpec((seq, SSM_INNER), lambda i: (blk0 + i, U_Z // SSM_INNER)),
                pl.BlockSpec((seq, DT_LANES), lambda i: (blk0 + i, U_DT // DT_LANES)),
                pl.BlockSpec((SUBLANES, SSM_XBC), lambda i: (0, 0)),
                vec(SSM_XBC), vec(DT_LANES), vec(DT_LANES), vec(SSM_INNER), vec(SSM_INNER)]
    args = [u, u, u, *weights]
    y_spec = pl.BlockSpec((seq, SSM_INNER), lambda i: (blk0 + i, 0))
    y_shape = jax.ShapeDtypeStruct((N_TOK, SSM_INNER), jnp.bfloat16)
    if latent:
        in_specs += [pl.BlockSpec((None,) + state_shape, lambda i: (i, 0, 0, 0)), pl.BlockSpec(memory_space=pl.ANY)]
        args += [h0, y_d]
        out_specs, out_shape, aliases = y_spec, y_shape, {len(args) - 1: 0}
    else:
        out_specs = [y_spec, pl.BlockSpec((None,) + state_shape, lambda i: (i, 0, 0, 0))]
        out_shape = [y_shape, jax.ShapeDtypeStruct((n_seq,) + state_shape, jnp.float32)]
        aliases = {}
    f32 = jnp.float32
    return pl.pallas_call(
        functools.partial(_ssd_kernel, seq=seq, latent=latent),
        grid=(n_seq,),
        in_specs=in_specs,
        out_specs=out_specs,
        out_shape=out_shape,
        scratch_shapes=[pltpu.VMEM((seq + 2 * SUBLANES, SSM_XBC), f32), pltpu.VMEM((seq, SSM_INNER), f32),
                        pltpu.VMEM((seq, 2 * SSM_GROUPS * SSM_STATE), f32), pltpu.VMEM((seq, DT_LANES), f32),
                        pltpu.VMEM((seq, SSM_INNER), f32), pltpu.VMEM(state_shape, f32)],
        input_output_aliases=aliases,
        compiler_params=_params(("arbitrary",)),
        name="ssd_latent" if latent else "ssd_ctx",
    )(*args)


def ssd_mixer(u, conv_w, conv_b, a_log, dt_bias, d_skip, norm_g, state_l):
    f32 = jnp.float32
    lane_pad = lambda v: jnp.pad(v.reshape(1, -1).astype(f32), ((0, 0), (0, DT_LANES - v.size)))
    weights = (jnp.pad(conv_w.astype(f32), ((0, SUBLANES - SSM_CONV), (0, 0))), conv_b.reshape(1, -1),
               lane_pad(dt_bias), lane_pad(a_log), jnp.repeat(d_skip, SSM_HEADDIM).reshape(1, -1),
               norm_g.reshape(1, -1))
    to_t = lambda s: s.transpose(0, 1, 4, 2, 3).reshape(s.shape[0], 2, SSM_STATE, SSM_INNER)
    y_d, h_fin_t = _ssd_call(u, weights, SEQ, BATCH, 0)
    y_d = _ssd_call(u, weights, DEC_SEQ, DEC_BATCH, LAT_BLK0, h0=to_t(state_l), y_d=y_d)
    h_fin = h_fin_t.reshape(BATCH, 2, SSM_STATE, SSM_HEADS, SSM_HEADDIM).transpose(0, 1, 3, 4, 2)
    return y_d, h_fin


def _pack_w_in(w_in_l):
    sizes = (MIX_WIDTH, B_HEADS * HD, B_KV_HEADS * HD, B_KV_HEADS * HD, C_HEADS * HD, C_HEADS * HD,
             C_HEADS * HD, SSM_INNER, SSM_XBC, 2 * SSM_HEADS, N_BRANCH * D_MODEL)
    offs = np.concatenate([[0], np.cumsum(sizes)]).tolist()
    pool, qb, kb, vb, qc, kc, vc, z, xbc, dt, gates = [w_in_l[:, offs[i]:offs[i + 1]] for i in range(len(sizes))]
    zeros = lambda n: jnp.zeros((D_MODEL, n), w_in_l.dtype)
    w_u = jnp.concatenate([qb, qc, kc, vc, pool, xbc, kb, vb, dt, zeros(U_Z - U_DT - 2 * SSM_HEADS), z], axis=1)
    w_gate = gates.reshape(D_MODEL, N_BRANCH, D_MODEL).transpose(1, 0, 2)
    return w_u.astype(jnp.bfloat16), w_gate.astype(jnp.bfloat16)


def kernel(x_prompt, x_sample, cache_b_k, cache_b_v, cache_c_k, cache_c_v, state_ssm, c, c_ctx, w_ada, b_ada, w_in, w_pool, pool_scale, attn_sink, na_rpb, conv_w, conv_b, a_log, dt_bias, d_skip, ssm_norm, w_branch, w_o, ln1_g, ln1_b, w_mlp1, w_mlp2, ln2_g, ln2_b):
    f32 = jnp.float32
    bf16 = jnp.bfloat16

    cond = jnp.concatenate([c_ctx[None, :], c, jnp.zeros((COND_PAD - N_COND, D_MODEL), f32)], axis=0)
    mod_all = ada_modulation(cond, w_ada, b_ada.reshape(DEPTH, 1, 6 * D_MODEL))
    blk_row = np.concatenate([np.zeros(N_CTX_TOK // MOD_ROWS, np.int32), 1 + np.arange(DEC_BATCH, dtype=np.int32)])
    mods = mod_all[:, blk_row].reshape(DEPTH, N_MOD_BLOCKS, 6, 1, D_MODEL).transpose(0, 2, 1, 3, 4)
    SHIFT1, SCALE1, GATE1, SHIFT2, SCALE2, GATE2 = range(6)

    x = jnp.concatenate([x_prompt.reshape(N_CTX_TOK, D_MODEL), x_sample.reshape(N_LAT_TOK, D_MODEL)], axis=0)
    h = modulate(x, mods[0], SCALE1, SHIFT1)
    rope_tables = _rope_tables()
    na_tables = _na_bias_tables(na_rpb)

    bk, bv, ckk, cvv, hs = [], [], [], [], []
    for l in range(DEPTH):
        w_u, w_gate = _pack_w_in(w_in[l])
        u = matmul(h, w_u, bf16, name="in_proj")

        y_a = pool_mix(u, w_pool[l].astype(bf16), pool_scale[l][None])
        y_a = pool_mix(u, w_pool[l].astype(bf16), pool_scale[l][None], y_a)
        y_b, y_c = ctx_attention(u, attn_sink[l])
        y_b = band_attention(u, y_b, cache_b_k, cache_b_v, l, attn_sink[l], rope_tables)
        y_c = na_attention(u, y_c, cache_c_k, cache_c_v, l, na_tables)
        y_d, h_ctx = ssd_mixer(u, conv_w[l], conv_b[l], a_log[l], dt_bias[l], d_skip[l], ssm_norm[l],
                               state_ssm[:, l])

        def ctx_heads(col, heads):
            return u[:N_CTX_TOK, col:col + heads * HD].astype(f32).reshape(BATCH, SEQ, heads, HD)

        bk.append(ctx_heads(U_KB, B_KV_HEADS))
        bv.append(ctx_heads(U_VB, B_KV_HEADS))
        ckk.append(ctx_heads(U_KC, C_HEADS))
        cvv.append(ctx_heads(U_VC, C_HEADS))
        hs.append(h_ctx)

        merged = merge_branches(h, (y_a, y_b, y_c, y_d), w_gate, w_branch[l].astype(bf16))
        x, h2 = matmul_ln(merged, w_o[l].astype(bf16), x, mods[l], GATE1, ln1_g[l][None], ln1_b[l][None],
                          mods[l], SCALE2, SHIFT2, name="out_proj_ln1")
        ff = matmul(h2, w_mlp1[l].astype(bf16), bf16, relu_sq=True, name="mlp_up")
        nxt = min(l + 1, DEPTH - 1)
        x, h = matmul_ln(ff, w_mlp2[l].astype(bf16), x, mods[l], GATE2, ln2_g[l][None], ln2_b[l][None],
                         mods[nxt], SCALE1, SHIFT1, name="mlp_down_ln2")

    y_prompt = x[:N_CTX_TOK].reshape(BATCH, SEQ, D_MODEL)
    y_sample = x[N_CTX_TOK:].reshape(DEC_BATCH, DEC_SEQ, D_MODEL)
    return (y_prompt, y_sample, jnp.stack(bk, axis=1), jnp.stack(bv, axis=1), jnp.stack(ckk, axis=1),
            jnp.stack(cvv, axis=1), jnp.stack(hs, axis=1))
```

```python
import functools

import jax
import jax.numpy as jnp
import numpy as np
from jax import lax
from jax.experimental import pallas as pl
from jax.experimental.pallas import tpu as pltpu

D_MODEL = 2048
BATCH = 32
SEQ = 256
DEPTH = 4
DEC_BATCH = 8
DEC_SEQ = 1024
PAST_LEN = 256
GRID_W = 64
HD = 128
MIX_WIDTH = 1024
N_BRANCH = 4
POOL_GROUPS = 4
POOL_GROUP_DIM = MIX_WIDTH // POOL_GROUPS
POOL_WINDOWS = (2, 4, 8, 16)
B_HEADS = 8
B_KV_HEADS = 2
GQA_GROUP = B_HEADS // B_KV_HEADS
B_WINDOW = 128
B_BLOCK = 128
C_HEADS = 8
NA_KH = 8
NA_KW = 16
SSM_HEADS = 16
SSM_HEADDIM = 64
SSM_INNER = SSM_HEADS * SSM_HEADDIM
SSM_GROUPS = 2
SSM_STATE = 64
SSM_CONV = 5
SSM_CHUNK = 128
SSM_XBC = SSM_INNER + 2 * SSM_GROUPS * SSM_STATE
D_FF = 4 * D_MODEL
ROPE_BASE = 10000.0
LN_EPS = 1e-5
RMS_EPS = 1e-6
NEG_INF = -1e30
ALPHA = (2 * DEPTH) ** 0.25
ATTN_SCALE = HD ** -0.5

N_CTX_TOK = BATCH * SEQ
N_LAT_TOK = DEC_BATCH * DEC_SEQ
N_TOK = N_CTX_TOK + N_LAT_TOK
MOD_ROWS = DEC_SEQ
N_MOD_BLOCKS = N_TOK // MOD_ROWS
N_COND = 1 + DEC_BATCH
COND_PAD = 16
LAT_BLK0 = N_CTX_TOK // DEC_SEQ
GRID_ROWS = DEC_SEQ // GRID_W

U_POOL = 0
U_QB = U_POOL + MIX_WIDTH
U_KB = U_QB + B_HEADS * HD
U_VB = U_KB + B_KV_HEADS * HD
U_QC = U_VB + B_KV_HEADS * HD
U_KC = U_QC + C_HEADS * HD
U_VC = U_KC + C_HEADS * HD
U_Z = U_VC + C_HEADS * HD
U_XBC = U_Z + SSM_INNER
U_DT = U_XBC + SSM_XBC
U_GATES = U_DT + 2 * SSM_HEADS
U_WIDTH = 8192

SUBLANES = 8
VMEM_LIMIT = 56 * 1024 * 1024
_NT = (((1,), (1,)), ((), ()))


def _params(sem):
    return pltpu.CompilerParams(dimension_semantics=sem, vmem_limit_bytes=VMEM_LIMIT)


def _ada_kernel(cond_ref, w_ref, b_ref, o_ref):
    cond = cond_ref[...]
    s = (cond * jax.nn.sigmoid(cond)).astype(jnp.bfloat16)
    acc = jnp.dot(s, w_ref[...].astype(jnp.bfloat16), preferred_element_type=jnp.float32)
    o_ref[...] = acc + b_ref[...]


def ada_modulation(cond, w_ada, b_ada):
    bn = 1024
    n = w_ada.shape[-1]
    return pl.pallas_call(
        _ada_kernel,
        grid=(DEPTH, n // bn),
        in_specs=[
            pl.BlockSpec((COND_PAD, D_MODEL), lambda l, j: (0, 0)),
            pl.BlockSpec((None, D_MODEL, bn), lambda l, j: (l, 0, j)),
            pl.BlockSpec((None, 1, bn), lambda l, j: (l, 0, j)),
        ],
        out_specs=pl.BlockSpec((None, COND_PAD, bn), lambda l, j: (l, 0, j)),
        out_shape=jax.ShapeDtypeStruct((DEPTH, COND_PAD, n), jnp.float32),
        compiler_params=_params(("arbitrary", "arbitrary")),
        name="ada_modulation",
    )(cond, w_ada, b_ada)


def _modulate_kernel(x_ref, scale_ref, shift_ref, o_ref):
    o_ref[...] = (x_ref[...] * (1.0 + scale_ref[...]) + shift_ref[...]).astype(o_ref.dtype)


def _mod_spec(which, bm, lag=0):
    return pl.BlockSpec((None, None, 1, D_MODEL),
                        lambda i, *_: (which, (jnp.maximum(i - lag, 0) * bm) // MOD_ROWS, 0, 0))


def modulate(x, mod, which_scale, which_shift):
    bm = 512
    return pl.pallas_call(
        _modulate_kernel,
        grid=(N_TOK // bm,),
        in_specs=[
            pl.BlockSpec((bm, D_MODEL), lambda i: (i, 0)),
            _mod_spec(which_scale, bm),
            _mod_spec(which_shift, bm),
        ],
        out_specs=pl.BlockSpec((bm, D_MODEL), lambda i: (i, 0)),
        out_shape=jax.ShapeDtypeStruct((N_TOK, D_MODEL), jnp.bfloat16),
        compiler_params=_params(("arbitrary",)),
        name="modulate",
    )(x, mod, mod)


def _matmul_kernel(x_ref, w_ref, o_ref, *, relu_sq):
    acc = jnp.dot(x_ref[...], w_ref[...], preferred_element_type=jnp.float32)
    if relu_sq:
        acc = jnp.square(jnp.maximum(acc, 0.0))
    o_ref[...] = acc.astype(o_ref.dtype)


def matmul(x, w, out_dtype, relu_sq=False, bm=1024, bn=2048, name="matmul"):
    m, k = x.shape
    n = w.shape[1]
    return pl.pallas_call(
        functools.partial(_matmul_kernel, relu_sq=relu_sq),
        grid=(m // bm, n // bn),
        in_specs=[
            pl.BlockSpec((bm, k), lambda i, j: (i, 0)),
            pl.BlockSpec((k, bn), lambda i, j: (0, j)),
        ],
        out_specs=pl.BlockSpec((bm, bn), lambda i, j: (i, j)),
        out_shape=jax.ShapeDtypeStruct((m, n), out_dtype),
        compiler_params=_params(("arbitrary", "arbitrary")),
        name=name,
    )(x, w)


def _merge_kernel(h_ref, *refs):
    y_refs, wg_refs, (wb_ref, o_ref) = refs[:N_BRANCH], refs[N_BRANCH:2 * N_BRANCH], refs[2 * N_BRANCH:]
    h = h_ref[...]
    acc = None
    for n in range(N_BRANCH):
        gate = jax.nn.sigmoid(jnp.dot(h, wg_refs[n][...], preferred_element_type=jnp.float32))
        proj = jnp.dot(y_refs[n][...], wb_ref[n], preferred_element_type=jnp.float32)
        acc = gate * proj if acc is None else acc + gate * proj
    o_ref[...] = acc.astype(o_ref.dtype)


def merge_branches(h, ys, w_gate, w_branch):
    bm, bn = 512, 512
    y_spec = pl.BlockSpec((bm, MIX_WIDTH), lambda i, j: (i, 0))
    gate_spec = lambda n: pl.BlockSpec((D_MODEL, bn), lambda i, j: (0, n * (D_MODEL // bn) + j))
    return pl.pallas_call(
        _merge_kernel,
        grid=(N_TOK // bm, D_MODEL // bn),
        in_specs=[
            pl.BlockSpec((bm, D_MODEL), lambda i, j: (i, 0)),
            y_spec, y_spec, y_spec, y_spec,
            *[gate_spec(n) for n in range(N_BRANCH)],
            pl.BlockSpec((N_BRANCH, MIX_WIDTH, bn), lambda i, j: (0, 0, j)),
        ],
        out_specs=pl.BlockSpec((bm, bn), lambda i, j: (i, j)),
        out_shape=jax.ShapeDtypeStruct((N_TOK, D_MODEL), jnp.bfloat16),
        compiler_params=_params(("arbitrary", "arbitrary")),
        name="merge_branches",
    )(h, *ys, *([w_gate] * N_BRANCH), w_branch)


def _matmul_ln_kernel(a_ref, w_ref, res_ref, gate_ref, g_ref, b_ref, scale_ref, shift_ref,
                      x_out_ref, h_out_ref, acc_even, acc_odd, *, nk):
    i = pl.program_id(0)
    k = pl.program_id(1)
    rows_per_step = acc_even.shape[0] // nk

    @pl.when((i == 0) & (k == 0))
    def _():
        acc_odd[...] = jnp.zeros(acc_odd.shape, jnp.float32)

    def step(acc_cur, acc_prev):
        part = jnp.dot(a_ref[...], w_ref[...], preferred_element_type=jnp.float32)
        if nk > 1:
            part = part + jnp.where(k > 0, acc_cur[...], 0.0)
        acc_cur[...] = part
        rows = pl.ds(pl.multiple_of(k * rows_per_step, rows_per_step), rows_per_step)
        r = ALPHA * res_ref[rows, :] + gate_ref[...] * acc_prev[rows, :]
        mu = jnp.mean(r, axis=-1, keepdims=True)
        d = r - mu
        var = jnp.mean(jnp.square(d), axis=-1, keepdims=True)
        xn = d * lax.rsqrt(var + LN_EPS) * g_ref[...] + b_ref[...]
        x_out_ref[rows, :] = xn
        h_out_ref[rows, :] = (xn * (1.0 + scale_ref[...]) + shift_ref[...]).astype(h_out_ref.dtype)

    @pl.when(i % 2 == 0)
    def _():
        step(acc_even, acc_odd)

    @pl.when(i % 2 == 1)
    def _():
        step(acc_odd, acc_even)


def matmul_ln(a, w, res, mod, which_gate, ln_g, ln_b, mod_next, which_scale, which_shift, name):
    m, kdim = a.shape
    bm, bk = 512, 2048
    nm, nk = m // bm, kdim // bk
    row = pl.BlockSpec((bm, D_MODEL), lambda i, k: (jnp.maximum(i - 1, 0), 0))
    vec = pl.BlockSpec((1, D_MODEL), lambda i, k: (0, 0))
    acc = pltpu.VMEM((bm, D_MODEL), jnp.float32)
    return pl.pallas_call(
        functools.partial(_matmul_ln_kernel, nk=nk),
        grid=(nm + 1, nk),
        in_specs=[
            pl.BlockSpec((bm, bk), lambda i, k: (jnp.minimum(i, nm - 1), k)),
            pl.BlockSpec((bk, D_MODEL), lambda i, k: (k, 0)),
            row,
            _mod_spec(which_gate, bm, lag=1),
            vec, vec,
            _mod_spec(which_scale, bm, lag=1),
            _mod_spec(which_shift, bm, lag=1),
        ],
        out_specs=[row, row],
        out_shape=[jax.ShapeDtypeStruct((m, D_MODEL), jnp.float32),
                   jax.ShapeDtypeStruct((m, D_MODEL), jnp.bfloat16)],
        scratch_shapes=[acc, acc],
        compiler_params=_params(("arbitrary", "arbitrary")),
        name=name,
    )(a, w, res, mod, ln_g, ln_b, mod_next, mod_next)


def _softmax_pv(parts, extra_logit=None):
    m = None
    for s, _ in parts:
        mi = jnp.max(s, axis=-1, keepdims=True)
        m = mi if m is None else jnp.maximum(m, mi)
    if extra_logit is not None:
        m = jnp.maximum(m, extra_logit)
    es = [jnp.exp(s - m) for s, _ in parts]
    denom = None
    for e in es:
        li = jnp.sum(e, axis=-1, keepdims=True)
        denom = li if denom is None else denom + li
    if extra_logit is not None:
        denom = denom + jnp.exp(extra_logit - m)
    inv = 1.0 / denom
    out = None
    for e, (_, v) in zip(es, parts):
        o = jnp.dot((e * inv).astype(jnp.bfloat16), v, preferred_element_type=jnp.float32)
        out = o if out is None else out + o
    return out


def _qk(q, k):
    return lax.dot_general(q, k, _NT, preferred_element_type=jnp.float32) * ATTN_SCALE


def _head(ref, h):
    return ref[:, h * HD:(h + 1) * HD]


def _ctx_attn_kernel(sink_ref, qb_ref, kb_ref, vb_ref, qc_ref, kc_ref, vc_ref, yb_ref, yc_ref):
    bf16 = jnp.bfloat16
    for h in range(B_HEADS):
        kvh = h // GQA_GROUP
        q = _head(qb_ref, h).astype(bf16)
        k = _head(kb_ref, kvh).astype(bf16)
        v = _head(vb_ref, kvh).astype(bf16)
        o = _softmax_pv([(_qk(q, k), v)], extra_logit=sink_ref[h])
        yb_ref[:, h * HD:(h + 1) * HD] = o.astype(yb_ref.dtype)
    for h in range(C_HEADS):
        q = _head(qc_ref, h).astype(bf16)
        k = _head(kc_ref, h).astype(bf16)
        v = _head(vc_ref, h).astype(bf16)
        o = _softmax_pv([(_qk(q, k), v)])
        yc_ref[:, h * HD:(h + 1) * HD] = o.astype(yc_ref.dtype)


def ctx_attention(u, sink):
    wide = lambda col: pl.BlockSpec((pl.Element(SEQ), pl.Element(MIX_WIDTH)), lambda i: (i * SEQ, col))
    kv_w = B_KV_HEADS * HD
    narrow = lambda col: pl.BlockSpec((SEQ, kv_w), lambda i: (i, col // kv_w))
    out = jax.ShapeDtypeStruct((N_TOK, MIX_WIDTH), jnp.bfloat16)
    return pl.pallas_call(
        _ctx_attn_kernel,
        grid=(BATCH,),
        in_specs=[pl.BlockSpec(memory_space=pltpu.SMEM),
                  wide(U_QB), narrow(U_KB), narrow(U_VB), wide(U_QC), wide(U_KC), wide(U_VC)],
        out_specs=[pl.BlockSpec((SEQ, MIX_WIDTH), lambda i: (i, 0))] * 2,
        out_shape=[out, out],
        compiler_params=_params(("arbitrary",)),
        name="ctx_attention",
    )(sink, u, u, u, u, u, u)


def _rope_tables():
    half = HD // 2
    quarter = half // 2
    t = np.arange(DEC_SEQ)
    freqs = ROPE_BASE ** (-jnp.arange(quarter, dtype=jnp.float32) / quarter)
    pos = jnp.stack([jnp.asarray(t // GRID_W), jnp.asarray(t % GRID_W)], axis=1).astype(jnp.float32)
    ang = pos[:, :, None] * freqs[None, None, :]
    cos = jnp.cos(ang)
    sin = jnp.sin(ang)
    zero = jnp.zeros_like(sin)
    cos_t = jnp.concatenate([cos, cos], axis=2).reshape(DEC_SEQ, HD)
    sin_a = jnp.concatenate([-sin, zero], axis=2).reshape(DEC_SEQ, HD)
    sin_b = jnp.concatenate([zero, sin], axis=2).reshape(DEC_SEQ, HD)
    return cos_t, sin_a, sin_b


def _band_attn_kernel(sink_ref, qb_ref, kb_ref, vb_ref, ck_ref, cv_ref, cos_ref, sa_ref, sb_ref, y_in_ref,
                      y_ref, q_scr, k_scr, v_scr):
    del y_in_ref
    bf16 = jnp.bfloat16
    quarter = HD // 4
    cos, sa, sb = cos_ref[...], sa_ref[...], sb_ref[...]

    def rope(x):
        return x * cos + pltpu.roll(x, HD - quarter, 1) * sa + pltpu.roll(x, quarter, 1) * sb

    for h in range(B_HEADS):
        q_scr[:, h * HD:(h + 1) * HD] = rope(_head(qb_ref, h).astype(jnp.float32)).astype(bf16)
    pad = jnp.zeros((B_WINDOW, B_KV_HEADS * HD), bf16)
    for scr in (k_scr, v_scr):
        scr[0:B_WINDOW, :] = pad
        scr[B_WINDOW + DEC_SEQ:, :] = pad
    for kvh in range(B_KV_HEADS):
        k_scr[B_WINDOW:B_WINDOW + DEC_SEQ, kvh * HD:(kvh + 1) * HD] = rope(_head(kb_ref, kvh).astype(jnp.float32)).astype(bf16)
    v_scr[B_WINDOW:B_WINDOW + DEC_SEQ, :] = vb_ref[...].astype(bf16)
    ck = ck_ref[...].astype(bf16)
    cv = cv_ref[...].astype(bf16)

    span = B_BLOCK + 2 * B_WINDOW
    rows = GQA_GROUP * B_BLOCK
    i_idx = lax.broadcasted_iota(jnp.int32, (rows, span), 0) & (B_BLOCK - 1)
    c_idx = lax.broadcasted_iota(jnp.int32, (rows, span), 1)
    rel = c_idx - i_idx
    band_ok = (rel >= 0) & (rel <= 2 * B_WINDOW)
    grp = lax.broadcasted_iota(jnp.int32, (rows, 1), 0) // B_BLOCK

    def block(n, carry):
        r0 = pl.multiple_of(n * B_BLOCK, B_BLOCK)
        kpos = c_idx + (r0 - B_WINDOW)
        valid = band_ok & (kpos >= 0) & (kpos < DEC_SEQ)
        for kvh in range(B_KV_HEADS):
            heads = [kvh * GQA_GROUP + g for g in range(GQA_GROUP)]
            q = jnp.concatenate([q_scr[pl.ds(r0, B_BLOCK), h * HD:(h + 1) * HD] for h in heads], axis=0)
            ks = k_scr[pl.ds(r0, span), kvh * HD:(kvh + 1) * HD]
            vs = v_scr[pl.ds(r0, span), kvh * HD:(kvh + 1) * HD]
            s_loc = jnp.where(valid, _qk(q, ks), NEG_INF)
            s_ctx = _qk(q, ck[:, kvh * HD:(kvh + 1) * HD])
            sink = jnp.zeros((rows, 1), jnp.float32)
            for g, h in enumerate(heads):
                sink = jnp.where(grp == g, sink_ref[h], sink)
            o = _softmax_pv([(s_loc, vs), (s_ctx, cv[:, kvh * HD:(kvh + 1) * HD])], extra_logit=sink)
            for g, h in enumerate(heads):
                y_ref[pl.ds(r0, B_BLOCK), h * HD:(h + 1) * HD] = o[g * B_BLOCK:(g + 1) * B_BLOCK].astype(y_ref.dtype)
        return carry

    lax.fori_loop(0, DEC_SEQ // B_BLOCK, block, 0)


def band_attention(u, y_b, cache_k, cache_v, layer, sink, rope_tables):
    kv_w = B_KV_HEADS * HD
    ck = cache_k.reshape(DEC_BATCH, DEPTH, PAST_LEN, kv_w)
    cv = cache_v.reshape(DEC_BATCH, DEPTH, PAST_LEN, kv_w)
    cache_spec = pl.BlockSpec((None, None, PAST_LEN, kv_w), lambda b: (b, layer, 0, 0))
    table_spec = pl.BlockSpec((DEC_SEQ, HD), lambda b: (0, 0))
    return pl.pallas_call(
        _band_attn_kernel,
        grid=(DEC_BATCH,),
        in_specs=[pl.BlockSpec(memory_space=pltpu.SMEM),
                  pl.BlockSpec((DEC_SEQ, MIX_WIDTH), lambda b: (LAT_BLK0 + b, U_QB // MIX_WIDTH)),
                  pl.BlockSpec((DEC_SEQ, kv_w), lambda b: (LAT_BLK0 + b, U_KB // kv_w)),
                  pl.BlockSpec((DEC_SEQ, kv_w), lambda b: (LAT_BLK0 + b, U_VB // kv_w)),
                  cache_spec, cache_spec, table_spec, table_spec, table_spec,
                  pl.BlockSpec(memory_space=pl.ANY)],
        out_specs=pl.BlockSpec((DEC_SEQ, MIX_WIDTH), lambda b: (LAT_BLK0 + b, 0)),
        out_shape=jax.ShapeDtypeStruct((N_TOK, MIX_WIDTH), y_b.dtype),
        scratch_shapes=[pltpu.VMEM((DEC_SEQ, MIX_WIDTH), jnp.bfloat16),
                        pltpu.VMEM((DEC_SEQ + 2 * B_WINDOW, kv_w), jnp.bfloat16),
                        pltpu.VMEM((DEC_SEQ + 2 * B_WINDOW, kv_w), jnp.bfloat16)],
        input_output_aliases={9: 0},
        compiler_params=_params(("arbitrary",)),
        name="band_attention",
    )(sink, u, u, u, ck, cv, *rope_tables, y_b)


def _na_bias_tables(na_rpb):
    col = np.arange(GRID_W)
    c0 = np.clip(col - NA_KW // 2, 0, GRID_W - NA_KW)
    col_ok = (col[None, :] >= c0[:, None]) & (col[None, :] < c0[:, None] + NA_KW)
    dc_i = np.clip(col[None, :] - col[:, None], -(NA_KW - 1), NA_KW - 1) + (NA_KW - 1)
    onehot = (np.arange(2 * NA_KW - 1)[:, None, None] == dc_i[None]).astype(np.float32)
    toeplitz = jnp.einsum('lhrc,cqk->lhrqk', na_rpb.astype(jnp.float32), onehot,
                          precision=lax.Precision.HIGHEST)
    toeplitz = jnp.where(col_ok, toeplitz, NEG_INF)
    masked = jnp.full((DEPTH, C_HEADS, GRID_W, GRID_W), NEG_INF, jnp.float32)
    row_blocks = []
    for rq in range(GRID_ROWS):
        first = _na_first_key_row(rq)
        pieces = [toeplitz[:, :, rk - rq + NA_KH - 1] if first <= rk < first + NA_KH else masked
                  for rk in range(GRID_ROWS)]
        row_blocks.append(jnp.concatenate(pieces, axis=-1))
    return jnp.concatenate(row_blocks, axis=-2)


def _na_first_key_row(rq):
    return min(max(rq - NA_KH // 2, 0), GRID_ROWS - NA_KH)


NA_Q_ROWS = 4


def _na_attn_kernel(q_ref, k_ref, v_ref, ck_ref, cv_ref, tab_ref, y_in_ref, y_ref):
    del y_in_ref
    bf16 = jnp.bfloat16
    ck = ck_ref[...].astype(bf16)
    cv = cv_ref[...].astype(bf16)
    for rq0 in range(0, GRID_ROWS, NA_Q_ROWS):
        lo = _na_first_key_row(rq0) * GRID_W // HD * HD
        hi = -(-(_na_first_key_row(rq0 + NA_Q_ROWS - 1) + NA_KH) * GRID_W // HD) * HD
        rows = slice(rq0 * GRID_W, (rq0 + NA_Q_ROWS) * GRID_W)
        q = q_ref[rows, :]
        s_loc = _qk(q, k_ref[lo:hi, :]) + tab_ref[rows, lo:hi]
        o = _softmax_pv([(s_loc, v_ref[lo:hi, :]), (_qk(q, ck), cv)])
        y_ref[rows, :] = o.astype(y_ref.dtype)


def na_attention(u, y_c, cache_k, cache_v, layer, bias_tables):
    assert u.dtype == jnp.bfloat16
    ck = cache_k.reshape(DEC_BATCH, DEPTH, PAST_LEN, C_HEADS * HD)
    cv = cache_v.reshape(DEC_BATCH, DEPTH, PAST_LEN, C_HEADS * HD)
    cache_spec = pl.BlockSpec((None, None, PAST_LEN, HD), lambda h, b: (b, layer, 0, h))
    head = lambda col: pl.BlockSpec((DEC_SEQ, HD), lambda h, b: (LAT_BLK0 + b, col // HD + h))
    return pl.pallas_call(
        _na_attn_kernel,
        grid=(C_HEADS, DEC_BATCH),
        in_specs=[head(U_QC), head(U_KC), head(U_VC), cache_spec, cache_spec,
                  pl.BlockSpec((None, None, DEC_SEQ, DEC_SEQ), lambda h, b: (layer, h, 0, 0)),
                  pl.BlockSpec(memory_space=pl.ANY)],
        out_specs=pl.BlockSpec((DEC_SEQ, HD), lambda h, b: (LAT_BLK0 + b, h)),
        out_shape=jax.ShapeDtypeStruct((N_TOK, MIX_WIDTH), y_c.dtype),
        input_output_aliases={6: 0},
        compiler_params=_params(("arbitrary", "arbitrary")),
        name="na_attention",
    )(u, u, u, ck, cv, bias_tables, y_c)


def _pool_kernel(a_ref, w_ref, scale_ref, *rest, seq):
    y_ref, pad_scr = rest[-2], rest[-1]
    halo = SUBLANES
    zeros = jnp.zeros((halo, MIX_WIDTH), jnp.float32)
    pad_scr[0:halo, :] = zeros
    pad_scr[halo + seq:, :] = zeros
    pad_scr[halo:halo + seq, :] = a_ref[...].astype(jnp.float32)
    t = lax.broadcasted_iota(jnp.int32, (seq, 1), 0)
    for g, w in enumerate(POOL_WINDOWS):
        cols = slice(g * POOL_GROUP_DIM, (g + 1) * POOL_GROUP_DIM)
        lo, hi = -(w // 2), w - w // 2
        total = None
        for d in range(lo, hi):
            term = pad_scr[pl.ds(halo + d, seq), cols]
            total = term if total is None else total + term
        cnt = (jnp.minimum(t + hi, seq) - jnp.maximum(t + lo, 0)).astype(jnp.float32)
        diff = total / cnt - pad_scr[halo:halo + seq, cols]
        y = jnp.dot(diff.astype(jnp.bfloat16), w_ref[g], preferred_element_type=jnp.float32)
        y_ref[:, cols] = (y * scale_ref[:, cols]).astype(y_ref.dtype)


def pool_mix(u, w_pool, pool_scale, y_a=None):
    latent = y_a is not None
    seq, n_seq, blk0 = (DEC_SEQ, DEC_BATCH, LAT_BLK0) if latent else (SEQ, BATCH, 0)
    in_specs = [pl.BlockSpec((seq, MIX_WIDTH), lambda i: (blk0 + i, U_POOL // MIX_WIDTH)),
                pl.BlockSpec((POOL_GROUPS, POOL_GROUP_DIM, POOL_GROUP_DIM), lambda i: (0, 0, 0)),
                pl.BlockSpec((1, MIX_WIDTH), lambda i: (0, 0))]
    args = [u, w_pool, pool_scale]
    aliases = {}
    if latent:
        in_specs.append(pl.BlockSpec(memory_space=pl.ANY))
        args.append(y_a)
        aliases = {3: 0}
    return pl.pallas_call(
        functools.partial(_pool_kernel, seq=seq),
        grid=(n_seq,),
        in_specs=in_specs,
        out_specs=pl.BlockSpec((seq, MIX_WIDTH), lambda i: (blk0 + i, 0)),
        out_shape=jax.ShapeDtypeStruct((N_TOK, MIX_WIDTH), jnp.bfloat16),
        scratch_shapes=[pltpu.VMEM((seq + 2 * SUBLANES, MIX_WIDTH), jnp.float32)],
        input_output_aliases=aliases,
        compiler_params=_params(("arbitrary",)),
        name="pool_mix_latent" if latent else "pool_mix_ctx",
    )(*args)


DT_LANES = 128


def _split_bf16(x, parts):
    out = []
    rest = x
    for _ in range(parts):
        p = rest.astype(jnp.bfloat16)
        out.append(p)
        rest = rest - p.astype(jnp.float32)
    return out


def _select_rows(sel, x):
    return sum(jnp.dot(sel, p, preferred_element_type=jnp.float32) for p in _split_bf16(x, 3))


def _select_cols(x, sel):
    return sum(jnp.dot(p, sel, preferred_element_type=jnp.float32) for p in _split_bf16(x, 2))


def _ssd_kernel(xbc_ref, z_ref, dtr_ref, convw_ref, convb_ref, dtb_ref, alog_ref, dskip_ref, g_ref, *rest,
                seq, latent):
    if latent:
        h0_ref, _, y_ref, pad_scr, xs_scr, bc_scr, dt_scr, yacc_scr, h_scr = rest
    else:
        y_ref, hfin_ref, pad_scr, xs_scr, bc_scr, dt_scr, yacc_scr, h_scr = rest
    f32, bf16 = jnp.float32, jnp.bfloat16
    Q = SSM_CHUNK
    nc = seq // Q
    halo = SUBLANES
    gw = SSM_INNER // SSM_GROUPS
    pair_w = 2 * SSM_HEADDIM

    zeros = jnp.zeros((halo, SSM_XBC), f32)
    pad_scr[0:halo, :] = zeros
    pad_scr[halo + seq:, :] = zeros
    pad_scr[halo:halo + seq, :] = xbc_ref[...].astype(f32)
    for c in range(nc):
        acc = convb_ref[...]
        for k in range(SSM_CONV):
            start = halo + c * Q + k - SSM_CONV // 2
            acc = acc + convw_ref[k:k + 1, :] * pad_scr[start:start + Q, :]
        act = acc * jax.nn.sigmoid(acc)
        xs_scr[c * Q:(c + 1) * Q, :] = act[:, :SSM_INNER]
        bc_scr[c * Q:(c + 1) * Q, :] = act[:, SSM_INNER:]
        yacc_scr[c * Q:(c + 1) * Q, :] = act[:, :SSM_INNER] * dskip_ref[...]

    dt_lane = lax.broadcasted_iota(jnp.int32, (1, DT_LANES), 1) < 2 * SSM_HEADS
    pre = jnp.where(dt_lane, dtr_ref[...].astype(f32) + dtb_ref[...], 0.0)
    dt_scr[...] = jnp.maximum(pre, 0.0) + jnp.log1p(jnp.exp(-jnp.abs(pre)))
    a = -jnp.exp(alog_ref[...])

    if latent:
        h_scr[...] = h0_ref[...]
    else:
        h_scr[...] = jnp.zeros(h_scr.shape, f32)

    ri = lax.broadcasted_iota(jnp.int32, (Q, Q), 0)
    ci = lax.broadcasted_iota(jnp.int32, (Q, Q), 1)
    causal = (ci <= ri, ci >= ri)
    tri = tuple(m.astype(bf16) for m in causal)
    er = lax.broadcasted_iota(jnp.int32, (DT_LANES, SSM_INNER), 0)
    ec = lax.broadcasted_iota(jnp.int32, (DT_LANES, SSM_INNER), 1) // SSM_HEADDIM
    expand = tuple((er == ec + d * SSM_HEADS).astype(bf16) for d in range(2))
    lane = lax.broadcasted_iota(jnp.int32, (Q, pair_w), 1)
    first_head = lane < SSM_HEADDIM

    def step(s, carry):
        for d in range(2):
            c = s if d == 0 else nc - 1 - s
            r0 = pl.multiple_of(c * Q, Q)
            dtc = dt_scr[pl.ds(r0, Q), :]
            cum = _select_rows(tri[d], dtc * a)
            last = cum[Q - 1:Q, :] if d == 0 else cum[0:1, :]
            w_state = jnp.exp(last - cum) * dtc
            cum_t = cum.T
            dt_t = dtc.T
            w_state_x = _select_cols(w_state, expand[d])
            ecum_x = _select_cols(jnp.exp(cum), expand[d])
            decay_x = _select_cols(jnp.broadcast_to(jnp.exp(last), (SUBLANES, DT_LANES)), expand[d])[0:1, :]
            bcv = bc_scr[pl.ds(r0, Q), :]
            bmat = bcv[:, :SSM_GROUPS * SSM_STATE]
            cmat = bcv[:, SSM_GROUPS * SSM_STATE:]
            bmat_t = bmat.T
            xs_c = xs_scr[pl.ds(r0, Q), :]
            xw = (xs_c * w_state_x).astype(bf16)
            for g in range(SSM_GROUPS):
                st = slice(g * SSM_STATE, (g + 1) * SSM_STATE)
                ch = slice(g * gw, (g + 1) * gw)
                bg = bmat[:, st].astype(bf16)
                cg = cmat[:, st].astype(bf16)
                cb = lax.dot_general(cg, bg, _NT, preferred_element_type=f32)
                h_prev = h_scr[d, :, ch]
                y_inter = jnp.dot(cg, h_prev.astype(bf16), preferred_element_type=f32) * ecum_x[:, ch]
                new_state = jnp.dot(bmat_t[st, :].astype(bf16), xw[:, ch], preferred_element_type=f32)
                h_scr[d, :, ch] = h_prev * decay_x[:, ch] + new_state
                for k in range(gw // pair_w):
                    head = (g * gw + k * pair_w) // SSM_HEADDIM
                    w_pair = []
                    for ln in (d * SSM_HEADS + head, d * SSM_HEADS + head + 1):
                        col = jnp.broadcast_to(cum[:, ln:ln + 1], (Q, Q))
                        seg = jnp.exp(jnp.where(causal[d], col - cum_t[ln:ln + 1, :], NEG_INF))
                        w_pair.append(cb * seg * dt_t[ln:ln + 1, :])
                    lhs = jnp.concatenate(w_pair, axis=1).astype(bf16)
                    pc = slice(g * gw + k * pair_w, g * gw + (k + 1) * pair_w)
                    xp = xs_c[:, pc]
                    rhs = jnp.concatenate([jnp.where(first_head, xp, 0.0), jnp.where(first_head, 0.0, xp)],
                                          axis=0).astype(bf16)
                    y_pair = jnp.dot(lhs, rhs, preferred_element_type=f32) + y_inter[:, k * pair_w:(k + 1) * pair_w]
                    yacc_scr[pl.ds(r0, Q), pc] += y_pair
        return carry

    lax.fori_loop(0, nc, step, 0)

    for c in range(nc):
        rows = slice(c * Q, (c + 1) * Q)
        zc = z_ref[rows, :].astype(f32)
        y = yacc_scr[rows, :] * (zc * jax.nn.sigmoid(zc))
        ms = jnp.mean(jnp.square(y), axis=-1, keepdims=True)
        y_ref[rows, :] = (y * lax.rsqrt(ms + RMS_EPS) * g_ref[...]).astype(y_ref.dtype)
    if not latent:
        hfin_ref[...] = h_scr[...]


def _ssd_call(u, weights, seq, n_seq, blk0, h0=None, y_d=None):
    latent = h0 is not None
    state_shape = (2, SSM_STATE, SSM_INNER)
    vec = lambda n: pl.BlockSpec((1, n), lambda i: (0, 0))
    window = lambda col, width: pl.BlockSpec((pl.Element(seq), pl.Element(width)),
                                             lambda i: ((blk0 + i) * seq, col))
    in_specs = [window(U_XBC, SSM_XBC), window(U_Z, SSM_INNER),
                pl.BlockSpec((seq, DT_LANES), lambda i: (blk0 + i, U_DT // DT_LANES)),
                pl.BlockSpec((SUBLANES, SSM_XBC), lambda i: (0, 0)),
                vec(SSM_XBC), vec(DT_LANES), vec(DT_LANES), vec(SSM_INNER), vec(SSM_INNER)]
    args = [u, u, u, *weights]
    y_spec = pl.BlockSpec((seq, SSM_INNER), lambda i: (blk0 + i, 0))
    y_shape = jax.ShapeDtypeStruct((N_TOK, SSM_INNER), jnp.bfloat16)
    if latent:
        in_specs += [pl.BlockSpec((None,) + state_shape, lambda i: (i, 0, 0, 0)), pl.BlockSpec(memory_space=pl.ANY)]
        args += [h0, y_d]
        out_specs, out_shape, aliases = y_spec, y_shape, {len(args) - 1: 0}
    else:
        out_specs = [y_spec, pl.BlockSpec((None,) + state_shape, lambda i: (i, 0, 0, 0))]
        out_shape = [y_shape, jax.ShapeDtypeStruct((n_seq,) + state_shape, jnp.float32)]
        aliases = {}
    f32 = jnp.float32
    return pl.pallas_call(
        functools.partial(_ssd_kernel, seq=seq, latent=latent),
        grid=(n_seq,),
        in_specs=in_specs,
        out_specs=out_specs,
        out_shape=out_shape,
        scratch_shapes=[pltpu.VMEM((seq + 2 * SUBLANES, SSM_XBC), f32), pltpu.VMEM((seq, SSM_INNER), f32),
                        pltpu.VMEM((seq, 2 * SSM_GROUPS * SSM_STATE), f32), pltpu.VMEM((seq, DT_LANES), f32),
                        pltpu.VMEM((seq, SSM_INNER), f32), pltpu.VMEM(state_shape, f32)],
        input_output_aliases=aliases,
        compiler_params=_params(("arbitrary",)),
        name="ssd_latent" if latent else "ssd_ctx",
    )(*args)


def ssd_mixer(u, conv_w, conv_b, a_log, dt_bias, d_skip, norm_g, state_l):
    f32 = jnp.float32
    lane_pad = lambda v: jnp.pad(v.reshape(1, -1).astype(f32), ((0, 0), (0, DT_LANES - v.size)))
    weights = (jnp.pad(conv_w.astype(f32), ((0, SUBLANES - SSM_CONV), (0, 0))), conv_b.reshape(1, -1),
               lane_pad(dt_bias), lane_pad(a_log), jnp.repeat(d_skip, SSM_HEADDIM).reshape(1, -1),
               norm_g.reshape(1, -1))
    to_t = lambda s: s.transpose(0, 1, 4, 2, 3).reshape(s.shape[0], 2, SSM_STATE, SSM_INNER)
    y_d, h_fin_t = _ssd_call(u, weights, SEQ, BATCH, 0)
    y_d = _ssd_call(u, weights, DEC_SEQ, DEC_BATCH, LAT_BLK0, h0=to_t(state_l), y_d=y_d)
    h_fin = h_fin_t.reshape(BATCH, 2, SSM_STATE, SSM_HEADS, SSM_HEADDIM).transpose(0, 1, 3, 4, 2)
    return y_d, h_fin


def _pack_w_in(w_in_l):
    return w_in_l[:, :U_WIDTH].astype(jnp.bfloat16), w_in_l[:, U_GATES:].astype(jnp.bfloat16)


def kernel(x_prompt, x_sample, cache_b_k, cache_b_v, cache_c_k, cache_c_v, state_ssm, c, c_ctx, w_ada, b_ada, w_in, w_pool, pool_scale, attn_sink, na_rpb, conv_w, conv_b, a_log, dt_bias, d_skip, ssm_norm, w_branch, w_o, ln1_g, ln1_b, w_mlp1, w_mlp2, ln2_g, ln2_b):
    f32 = jnp.float32
    bf16 = jnp.bfloat16

    cond = jnp.concatenate([c_ctx[None, :], c, jnp.zeros((COND_PAD - N_COND, D_MODEL), f32)], axis=0)
    mod_all = ada_modulation(cond, w_ada, b_ada.reshape(DEPTH, 1, 6 * D_MODEL))
    blk_row = np.concatenate([np.zeros(N_CTX_TOK // MOD_ROWS, np.int32), 1 + np.arange(DEC_BATCH, dtype=np.int32)])
    mods = mod_all[:, blk_row].reshape(DEPTH, N_MOD_BLOCKS, 6, 1, D_MODEL).transpose(0, 2, 1, 3, 4)
    SHIFT1, SCALE1, GATE1, SHIFT2, SCALE2, GATE2 = range(6)

    x = jnp.concatenate([x_prompt.reshape(N_CTX_TOK, D_MODEL), x_sample.reshape(N_LAT_TOK, D_MODEL)], axis=0)
    h = modulate(x, mods[0], SCALE1, SHIFT1)
    rope_tables = _rope_tables()
    na_tables = _na_bias_tables(na_rpb)

    bk, bv, ckk, cvv, hs = [], [], [], [], []
    for l in range(DEPTH):
        w_u, w_gate = _pack_w_in(w_in[l])
        u = matmul(h, w_u, bf16, name="in_proj")

        y_a = pool_mix(u, w_pool[l].astype(bf16), pool_scale[l][None])
        y_a = pool_mix(u, w_pool[l].astype(bf16), pool_scale[l][None], y_a)
        y_b, y_c = ctx_attention(u, attn_sink[l])
        y_b = band_attention(u, y_b, cache_b_k, cache_b_v, l, attn_sink[l], rope_tables)
        y_c = na_attention(u, y_c, cache_c_k, cache_c_v, l, na_tables)
        y_d, h_ctx = ssd_mixer(u, conv_w[l], conv_b[l], a_log[l], dt_bias[l], d_skip[l], ssm_norm[l],
                               state_ssm[:, l])

        def ctx_heads(col, heads):
            return u[:N_CTX_TOK, col:col + heads * HD].astype(f32).reshape(BATCH, SEQ, heads, HD)

        bk.append(ctx_heads(U_KB, B_KV_HEADS))
        bv.append(ctx_heads(U_VB, B_KV_HEADS))
        ckk.append(ctx_heads(U_KC, C_HEADS))
        cvv.append(ctx_heads(U_VC, C_HEADS))
        hs.append(h_ctx)

        merged = merge_branches(h, (y_a, y_b, y_c, y_d), w_gate, w_branch[l].astype(bf16))
        x, h2 = matmul_ln(merged, w_o[l].astype(bf16), x, mods[l], GATE1, ln1_g[l][None], ln1_b[l][None],
                          mods[l], SCALE2, SHIFT2, name="out_proj_ln1")
        ff = matmul(h2, w_mlp1[l].astype(bf16), bf16, relu_sq=True, name="mlp_up")
        nxt = min(l + 1, DEPTH - 1)
        x, h = matmul_ln(ff, w_mlp2[l].astype(bf16), x, mods[l], GATE2, ln2_g[l][None], ln2_b[l][None],
                         mods[nxt], SCALE1, SHIFT1, name="mlp_down_ln2")

    y_prompt = x[:N_CTX_TOK].reshape(BATCH, SEQ, D_MODEL)
    y_sample = x[N_CTX_TOK:].reshape(DEC_BATCH, DEC_SEQ, D_MODEL)
    return (y_prompt, y_sample, jnp.stack(bk, axis=1), jnp.stack(bv, axis=1), jnp.stack(ckk, axis=1),
            jnp.stack(cvv, axis=1), jnp.stack(hs, axis=1))
```

```python
import functools

import jax
import jax.numpy as jnp
import numpy as np
from jax import lax
from jax.experimental import pallas as pl
from jax.experimental.pallas import tpu as pltpu

D_MODEL = 2048
BATCH = 32
SEQ = 256
DEPTH = 4
DEC_BATCH = 8
DEC_SEQ = 1024
PAST_LEN = 256
GRID_W = 64
HD = 128
MIX_WIDTH = 1024
N_BRANCH = 4
POOL_GROUPS = 4
POOL_GROUP_DIM = MIX_WIDTH // POOL_GROUPS
POOL_WINDOWS = (2, 4, 8, 16)
B_HEADS = 8
B_KV_HEADS = 2
GQA_GROUP = B_HEADS // B_KV_HEADS
B_WINDOW = 128
B_BLOCK = 128
C_HEADS = 8
NA_KH = 8
NA_KW = 16
SSM_HEADS = 16
SSM_HEADDIM = 64
SSM_INNER = SSM_HEADS * SSM_HEADDIM
SSM_GROUPS = 2
SSM_STATE = 64
SSM_CONV = 5
SSM_CHUNK = 128
SSM_XBC = SSM_INNER + 2 * SSM_GROUPS * SSM_STATE
D_FF = 4 * D_MODEL
ROPE_BASE = 10000.0
LN_EPS = 1e-5
RMS_EPS = 1e-6
NEG_INF = -1e30
ALPHA = (2 * DEPTH) ** 0.25
ATTN_SCALE = HD ** -0.5

N_CTX_TOK = BATCH * SEQ
N_LAT_TOK = DEC_BATCH * DEC_SEQ
N_TOK = N_CTX_TOK + N_LAT_TOK
MOD_ROWS = DEC_SEQ
N_MOD_BLOCKS = N_TOK // MOD_ROWS
N_COND = 1 + DEC_BATCH
COND_PAD = 16
LAT_BLK0 = N_CTX_TOK // DEC_SEQ
GRID_ROWS = DEC_SEQ // GRID_W

U_POOL = 0
U_QB = U_POOL + MIX_WIDTH
U_KB = U_QB + B_HEADS * HD
U_VB = U_KB + B_KV_HEADS * HD
U_QC = U_VB + B_KV_HEADS * HD
U_KC = U_QC + C_HEADS * HD
U_VC = U_KC + C_HEADS * HD
U_Z = U_VC + C_HEADS * HD
U_XBC = U_Z + SSM_INNER
U_DT = U_XBC + SSM_XBC
U_GATES = U_DT + 2 * SSM_HEADS
U_WIDTH = 8192

SUBLANES = 8
VMEM_LIMIT = 56 * 1024 * 1024
_NT = (((1,), (1,)), ((), ()))


def _params(sem):
    return pltpu.CompilerParams(dimension_semantics=sem, vmem_limit_bytes=VMEM_LIMIT)


def _ada_kernel(cond_ref, w_ref, b_ref, o_ref):
    cond = cond_ref[...]
    s = (cond * jax.nn.sigmoid(cond)).astype(jnp.bfloat16)
    acc = jnp.dot(s, w_ref[...].astype(jnp.bfloat16), preferred_element_type=jnp.float32)
    o_ref[...] = acc + b_ref[...]


def ada_modulation(cond, w_ada, b_ada):
    bn = 2048
    n = w_ada.shape[-1]
    return pl.pallas_call(
        _ada_kernel,
        grid=(DEPTH, n // bn),
        in_specs=[
            pl.BlockSpec((COND_PAD, D_MODEL), lambda l, j: (0, 0)),
            pl.BlockSpec((None, D_MODEL, bn), lambda l, j: (l, 0, j)),
            pl.BlockSpec((None, 1, bn), lambda l, j: (l, 0, j)),
        ],
        out_specs=pl.BlockSpec((None, COND_PAD, bn), lambda l, j: (l, 0, j)),
        out_shape=jax.ShapeDtypeStruct((DEPTH, COND_PAD, n), jnp.float32),
        compiler_params=_params(("arbitrary", "arbitrary")),
        name="ada_modulation",
    )(cond, w_ada, b_ada)


def _modulate_kernel(x_ref, scale_ref, shift_ref, o_ref):
    o_ref[...] = (x_ref[...] * (1.0 + scale_ref[...]) + shift_ref[...]).astype(o_ref.dtype)


def _mod_spec(which, bm, lag=0):
    return pl.BlockSpec((None, None, 1, D_MODEL),
                        lambda i, *_: (which, (jnp.maximum(i - lag, 0) * bm) // MOD_ROWS, 0, 0))


def modulate(x, mod, which_scale, which_shift):
    bm = 512
    return pl.pallas_call(
        _modulate_kernel,
        grid=(N_TOK // bm,),
        in_specs=[
            pl.BlockSpec((bm, D_MODEL), lambda i: (i, 0)),
            _mod_spec(which_scale, bm),
            _mod_spec(which_shift, bm),
        ],
        out_specs=pl.BlockSpec((bm, D_MODEL), lambda i: (i, 0)),
        out_shape=jax.ShapeDtypeStruct((N_TOK, D_MODEL), jnp.bfloat16),
        compiler_params=_params(("arbitrary",)),
        name="modulate",
    )(x, mod, mod)


def _matmul_kernel(x_ref, w_ref, o_ref, *, relu_sq):
    acc = jnp.dot(x_ref[...], w_ref[...], preferred_element_type=jnp.float32)
    if relu_sq:
        acc = jnp.square(jnp.maximum(acc, 0.0))
    o_ref[...] = acc.astype(o_ref.dtype)


def matmul(x, w, out_dtype, relu_sq=False, bm=1024, bn=2048, name="matmul"):
    m, k = x.shape
    n = w.shape[1]
    return pl.pallas_call(
        functools.partial(_matmul_kernel, relu_sq=relu_sq),
        grid=(m // bm, n // bn),
        in_specs=[
            pl.BlockSpec((bm, k), lambda i, j: (i, 0)),
            pl.BlockSpec((k, bn), lambda i, j: (0, j)),
        ],
        out_specs=pl.BlockSpec((bm, bn), lambda i, j: (i, j)),
        out_shape=jax.ShapeDtypeStruct((m, n), out_dtype),
        compiler_params=_params(("arbitrary", "arbitrary")),
        name=name,
    )(x, w)


def _merge_kernel(h_ref, *refs, n_ctx_blocks):
    y_ctx, y_lat = refs[:N_BRANCH], refs[N_BRANCH:2 * N_BRANCH]
    wg_refs, (wb_ref, o_ref) = refs[2 * N_BRANCH:3 * N_BRANCH], refs[3 * N_BRANCH:]

    def merge(y_refs):
        h = h_ref[...]
        acc = None
        for n in range(N_BRANCH):
            gate = jax.nn.sigmoid(jnp.dot(h, wg_refs[n][...], preferred_element_type=jnp.float32))
            proj = jnp.dot(y_refs[n][...], wb_ref[n], preferred_element_type=jnp.float32)
            acc = gate * proj if acc is None else acc + gate * proj
        o_ref[...] = acc.astype(o_ref.dtype)

    is_ctx = pl.program_id(0) < n_ctx_blocks
    pl.when(is_ctx)(lambda: merge(y_ctx))
    pl.when(jnp.logical_not(is_ctx))(lambda: merge(y_lat))


def merge_branches(h, ys_ctx, ys_lat, w_gate, w_branch):
    bm, bn = 512, 512
    n_ctx_blocks = N_CTX_TOK // bm
    ctx_spec = pl.BlockSpec((bm, MIX_WIDTH), lambda i, j: (jnp.minimum(i, n_ctx_blocks - 1), 0))
    lat_spec = pl.BlockSpec((bm, MIX_WIDTH), lambda i, j: (jnp.maximum(i - n_ctx_blocks, 0), 0))
    gate_spec = lambda n: pl.BlockSpec((D_MODEL, bn), lambda i, j: (0, n * (D_MODEL // bn) + j))
    return pl.pallas_call(
        functools.partial(_merge_kernel, n_ctx_blocks=n_ctx_blocks),
        grid=(N_TOK // bm, D_MODEL // bn),
        in_specs=[
            pl.BlockSpec((bm, D_MODEL), lambda i, j: (i, 0)),
            *[ctx_spec] * N_BRANCH, *[lat_spec] * N_BRANCH,
            *[gate_spec(n) for n in range(N_BRANCH)],
            pl.BlockSpec((N_BRANCH, MIX_WIDTH, bn), lambda i, j: (0, 0, j)),
        ],
        out_specs=pl.BlockSpec((bm, bn), lambda i, j: (i, j)),
        out_shape=jax.ShapeDtypeStruct((N_TOK, D_MODEL), jnp.bfloat16),
        compiler_params=_params(("arbitrary", "arbitrary")),
        name="merge_branches",
    )(h, *ys_ctx, *ys_lat, *([w_gate] * N_BRANCH), w_branch)


def _matmul_ln_kernel(a_ref, w_ref, res_ref, gate_ref, g_ref, b_ref, scale_ref, shift_ref,
                      x_out_ref, h_out_ref, acc_even, acc_odd, *, nk):
    i = pl.program_id(0)
    k = pl.program_id(1)
    rows_per_step = acc_even.shape[0] // nk

    @pl.when((i == 0) & (k == 0))
    def _():
        acc_odd[...] = jnp.zeros(acc_odd.shape, jnp.float32)

    def step(acc_cur, acc_prev):
        part = jnp.dot(a_ref[...], w_ref[...], preferred_element_type=jnp.float32)
        if nk > 1:
            part = part + jnp.where(k > 0, acc_cur[...], 0.0)
        acc_cur[...] = part
        rows = pl.ds(pl.multiple_of(k * rows_per_step, rows_per_step), rows_per_step)
        r = ALPHA * res_ref[rows, :] + gate_ref[...] * acc_prev[rows, :]
        mu = jnp.mean(r, axis=-1, keepdims=True)
        d = r - mu
        var = jnp.mean(jnp.square(d), axis=-1, keepdims=True)
        xn = d * lax.rsqrt(var + LN_EPS) * g_ref[...] + b_ref[...]
        x_out_ref[rows, :] = xn
        h_out_ref[rows, :] = (xn * (1.0 + scale_ref[...]) + shift_ref[...]).astype(h_out_ref.dtype)

    @pl.when(i % 2 == 0)
    def _():
        step(acc_even, acc_odd)

    @pl.when(i % 2 == 1)
    def _():
        step(acc_odd, acc_even)


def matmul_ln(a, w, res, mod, which_gate, ln_g, ln_b, mod_next, which_scale, which_shift, name):
    m, kdim = a.shape
    bm, bk = 512, 2048
    nm, nk = m // bm, kdim // bk
    row = pl.BlockSpec((bm, D_MODEL), lambda i, k: (jnp.maximum(i - 1, 0), 0))
    vec = pl.BlockSpec((1, D_MODEL), lambda i, k: (0, 0))
    acc = pltpu.VMEM((bm, D_MODEL), jnp.float32)
    return pl.pallas_call(
        functools.partial(_matmul_ln_kernel, nk=nk),
        grid=(nm + 1, nk),
        in_specs=[
            pl.BlockSpec((bm, bk), lambda i, k: (jnp.minimum(i, nm - 1), k)),
            pl.BlockSpec((bk, D_MODEL), lambda i, k: (k, 0)),
            row,
            _mod_spec(which_gate, bm, lag=1),
            vec, vec,
            _mod_spec(which_scale, bm, lag=1),
            _mod_spec(which_shift, bm, lag=1),
        ],
        out_specs=[row, row],
        out_shape=[jax.ShapeDtypeStruct((m, D_MODEL), jnp.float32),
                   jax.ShapeDtypeStruct((m, D_MODEL), jnp.bfloat16)],
        scratch_shapes=[acc, acc],
        compiler_params=_params(("arbitrary", "arbitrary")),
        name=name,
    )(a, w, res, mod, ln_g, ln_b, mod_next, mod_next)


def _softmax_pv(parts, extra_logit=None):
    m = None
    for s, _ in parts:
        mi = jnp.max(s, axis=-1, keepdims=True)
        m = mi if m is None else jnp.maximum(m, mi)
    if extra_logit is not None:
        m = jnp.maximum(m, extra_logit)
    es = [jnp.exp(s - m) for s, _ in parts]
    denom = None
    for e in es:
        li = jnp.sum(e, axis=-1, keepdims=True)
        denom = li if denom is None else denom + li
    if extra_logit is not None:
        denom = denom + jnp.exp(extra_logit - m)
    inv = 1.0 / denom
    out = None
    for e, (_, v) in zip(es, parts):
        o = jnp.dot((e * inv).astype(jnp.bfloat16), v, preferred_element_type=jnp.float32)
        out = o if out is None else out + o
    return out


def _qk(q, k):
    return lax.dot_general(q, k, _NT, preferred_element_type=jnp.float32) * ATTN_SCALE


def _head(ref, h):
    return ref[:, h * HD:(h + 1) * HD]


def _ctx_attn_kernel(sink_ref, qb_ref, kb_ref, vb_ref, qc_ref, kc_ref, vc_ref, yb_ref, yc_ref):
    bf16 = jnp.bfloat16
    for h in range(B_HEADS):
        kvh = h // GQA_GROUP
        q = _head(qb_ref, h).astype(bf16)
        k = _head(kb_ref, kvh).astype(bf16)
        v = _head(vb_ref, kvh).astype(bf16)
        o = _softmax_pv([(_qk(q, k), v)], extra_logit=sink_ref[h])
        yb_ref[:, h * HD:(h + 1) * HD] = o.astype(yb_ref.dtype)
    for h in range(C_HEADS):
        q = _head(qc_ref, h).astype(bf16)
        k = _head(kc_ref, h).astype(bf16)
        v = _head(vc_ref, h).astype(bf16)
        o = _softmax_pv([(_qk(q, k), v)])
        yc_ref[:, h * HD:(h + 1) * HD] = o.astype(yc_ref.dtype)


def ctx_attention(u, sink):
    wide = lambda col: pl.BlockSpec((pl.Element(SEQ), pl.Element(MIX_WIDTH)), lambda i: (i * SEQ, col))
    kv_w = B_KV_HEADS * HD
    narrow = lambda col: pl.BlockSpec((SEQ, kv_w), lambda i: (i, col // kv_w))
    out = jax.ShapeDtypeStruct((N_CTX_TOK, MIX_WIDTH), jnp.bfloat16)
    return pl.pallas_call(
        _ctx_attn_kernel,
        grid=(BATCH,),
        in_specs=[pl.BlockSpec(memory_space=pltpu.SMEM),
                  wide(U_QB), narrow(U_KB), narrow(U_VB), wide(U_QC), wide(U_KC), wide(U_VC)],
        out_specs=[pl.BlockSpec((SEQ, MIX_WIDTH), lambda i: (i, 0))] * 2,
        out_shape=[out, out],
        compiler_params=_params(("arbitrary",)),
        name="ctx_attention",
    )(sink, u, u, u, u, u, u)


def _rope_tables():
    half = HD // 2
    quarter = half // 2
    t = np.arange(DEC_SEQ)
    freqs = ROPE_BASE ** (-jnp.arange(quarter, dtype=jnp.float32) / quarter)
    pos = jnp.stack([jnp.asarray(t // GRID_W), jnp.asarray(t % GRID_W)], axis=1).astype(jnp.float32)
    ang = pos[:, :, None] * freqs[None, None, :]
    cos = jnp.cos(ang)
    sin = jnp.sin(ang)
    zero = jnp.zeros_like(sin)
    cos_t = jnp.concatenate([cos, cos], axis=2).reshape(DEC_SEQ, HD)
    sin_a = jnp.concatenate([-sin, zero], axis=2).reshape(DEC_SEQ, HD)
    sin_b = jnp.concatenate([zero, sin], axis=2).reshape(DEC_SEQ, HD)
    return cos_t, sin_a, sin_b


def _band_attn_kernel(sink_ref, qb_ref, kb_ref, vb_ref, ck_ref, cv_ref, cos_ref, sa_ref, sb_ref,
                      y_ref, q_scr, k_scr, v_scr):
    bf16 = jnp.bfloat16
    quarter = HD // 4
    cos, sa, sb = cos_ref[...], sa_ref[...], sb_ref[...]

    def rope(x):
        return x * cos + pltpu.roll(x, HD - quarter, 1) * sa + pltpu.roll(x, quarter, 1) * sb

    for h in range(B_HEADS):
        q_scr[:, h * HD:(h + 1) * HD] = rope(_head(qb_ref, h).astype(jnp.float32)).astype(bf16)
    pad = jnp.zeros((B_WINDOW, B_KV_HEADS * HD), bf16)
    for scr in (k_scr, v_scr):
        scr[0:B_WINDOW, :] = pad
        scr[B_WINDOW + DEC_SEQ:, :] = pad
    for kvh in range(B_KV_HEADS):
        k_scr[B_WINDOW:B_WINDOW + DEC_SEQ, kvh * HD:(kvh + 1) * HD] = rope(_head(kb_ref, kvh).astype(jnp.float32)).astype(bf16)
    v_scr[B_WINDOW:B_WINDOW + DEC_SEQ, :] = vb_ref[...].astype(bf16)
    ck = ck_ref[...].astype(bf16)
    cv = cv_ref[...].astype(bf16)

    span = B_BLOCK + 2 * B_WINDOW
    rows = GQA_GROUP * B_BLOCK
    i_idx = lax.broadcasted_iota(jnp.int32, (rows, span), 0) & (B_BLOCK - 1)
    c_idx = lax.broadcasted_iota(jnp.int32, (rows, span), 1)
    rel = c_idx - i_idx
    band_ok = (rel >= 0) & (rel <= 2 * B_WINDOW)
    grp = lax.broadcasted_iota(jnp.int32, (rows, 1), 0) // B_BLOCK

    def block(n, carry):
        r0 = pl.multiple_of(n * B_BLOCK, B_BLOCK)
        kpos = c_idx + (r0 - B_WINDOW)
        valid = band_ok & (kpos >= 0) & (kpos < DEC_SEQ)
        for kvh in range(B_KV_HEADS):
            heads = [kvh * GQA_GROUP + g for g in range(GQA_GROUP)]
            q = jnp.concatenate([q_scr[pl.ds(r0, B_BLOCK), h * HD:(h + 1) * HD] for h in heads], axis=0)
            ks = k_scr[pl.ds(r0, span), kvh * HD:(kvh + 1) * HD]
            vs = v_scr[pl.ds(r0, span), kvh * HD:(kvh + 1) * HD]
            s_loc = jnp.where(valid, _qk(q, ks), NEG_INF)
            s_ctx = _qk(q, ck[:, kvh * HD:(kvh + 1) * HD])
            sink = jnp.zeros((rows, 1), jnp.float32)
            for g, h in enumerate(heads):
                sink = jnp.where(grp == g, sink_ref[h], sink)
            o = _softmax_pv([(s_loc, vs), (s_ctx, cv[:, kvh * HD:(kvh + 1) * HD])], extra_logit=sink)
            for g, h in enumerate(heads):
                y_ref[pl.ds(r0, B_BLOCK), h * HD:(h + 1) * HD] = o[g * B_BLOCK:(g + 1) * B_BLOCK].astype(y_ref.dtype)
        return carry

    lax.fori_loop(0, DEC_SEQ // B_BLOCK, block, 0)


def band_attention(u, cache_k, cache_v, layer, sink, rope_tables):
    kv_w = B_KV_HEADS * HD
    ck = cache_k.reshape(DEC_BATCH, DEPTH, PAST_LEN, kv_w)
    cv = cache_v.reshape(DEC_BATCH, DEPTH, PAST_LEN, kv_w)
    cache_spec = pl.BlockSpec((None, None, PAST_LEN, kv_w), lambda b: (b, layer, 0, 0))
    table_spec = pl.BlockSpec((DEC_SEQ, HD), lambda b: (0, 0))
    return pl.pallas_call(
        _band_attn_kernel,
        grid=(DEC_BATCH,),
        in_specs=[pl.BlockSpec(memory_space=pltpu.SMEM),
                  pl.BlockSpec((DEC_SEQ, MIX_WIDTH), lambda b: (LAT_BLK0 + b, U_QB // MIX_WIDTH)),
                  pl.BlockSpec((DEC_SEQ, kv_w), lambda b: (LAT_BLK0 + b, U_KB // kv_w)),
                  pl.BlockSpec((DEC_SEQ, kv_w), lambda b: (LAT_BLK0 + b, U_VB // kv_w)),
                  cache_spec, cache_spec, table_spec, table_spec, table_spec],
        out_specs=pl.BlockSpec((DEC_SEQ, MIX_WIDTH), lambda b: (b, 0)),
        out_shape=jax.ShapeDtypeStruct((N_LAT_TOK, MIX_WIDTH), jnp.bfloat16),
        scratch_shapes=[pltpu.VMEM((DEC_SEQ, MIX_WIDTH), jnp.bfloat16),
                        pltpu.VMEM((DEC_SEQ + 2 * B_WINDOW, kv_w), jnp.bfloat16),
                        pltpu.VMEM((DEC_SEQ + 2 * B_WINDOW, kv_w), jnp.bfloat16)],
        compiler_params=_params(("arbitrary",)),
        name="band_attention",
    )(sink, u, u, u, ck, cv, *rope_tables)


def _na_bias_tables(na_rpb):
    col = np.arange(GRID_W)
    c0 = np.clip(col - NA_KW // 2, 0, GRID_W - NA_KW)
    col_ok = (col[None, :] >= c0[:, None]) & (col[None, :] < c0[:, None] + NA_KW)
    dc_i = np.clip(col[None, :] - col[:, None], -(NA_KW - 1), NA_KW - 1) + (NA_KW - 1)
    onehot = (np.arange(2 * NA_KW - 1)[:, None, None] == dc_i[None]).astype(np.float32)
    toeplitz = jnp.einsum('lhrc,cqk->lhrqk', na_rpb.astype(jnp.float32), onehot,
                          precision=lax.Precision.HIGHEST)
    toeplitz = jnp.where(col_ok, toeplitz, NEG_INF)
    masked = jnp.full((DEPTH, C_HEADS, GRID_W, GRID_W), NEG_INF, jnp.float32)
    blocks, _ = _na_query_blocks()
    tables = {}
    for rq0, lo, hi, off in blocks:
        if off in tables:
            continue
        rows = []
        for rq in range(rq0, rq0 + NA_Q_ROWS):
            first = _na_first_key_row(rq)
            pieces = [toeplitz[:, :, rk - rq + NA_KH - 1] if first <= rk < first + NA_KH else masked
                      for rk in range(lo // GRID_W, hi // GRID_W)]
            rows.append(jnp.concatenate(pieces, axis=-1))
        tables[off] = jnp.concatenate(rows, axis=-2)
    return jnp.concatenate([tables[off] for off in sorted(tables)], axis=-1)


def _na_first_key_row(rq):
    return min(max(rq - NA_KH // 2, 0), GRID_ROWS - NA_KH)


NA_Q_ROWS = 4


def _na_query_blocks():
    blocks, offsets, width = [], {}, 0
    for rq0 in range(0, GRID_ROWS, NA_Q_ROWS):
        lo = _na_first_key_row(rq0) * GRID_W // HD * HD
        hi = -(-(_na_first_key_row(rq0 + NA_Q_ROWS - 1) + NA_KH) * GRID_W // HD) * HD
        shape = (hi - lo, rq0 * GRID_W - lo) + tuple(_na_first_key_row(rq) * GRID_W - lo
                                                     for rq in range(rq0, rq0 + NA_Q_ROWS))
        if shape not in offsets:
            offsets[shape] = width
            width += hi - lo
        blocks.append((rq0, lo, hi, offsets[shape]))
    return blocks, width


def _na_attn_kernel(q_ref, k_ref, v_ref, ck_ref, cv_ref, tab_ref, y_ref):
    bf16 = jnp.bfloat16
    ck = ck_ref[...].astype(bf16)
    cv = cv_ref[...].astype(bf16)
    for rq0, lo, hi, off in _na_query_blocks()[0]:
        rows = slice(rq0 * GRID_W, (rq0 + NA_Q_ROWS) * GRID_W)
        q = q_ref[rows, :]
        s_loc = _qk(q, k_ref[lo:hi, :]) + tab_ref[:, off:off + hi - lo]
        o = _softmax_pv([(s_loc, v_ref[lo:hi, :]), (_qk(q, ck), cv)])
        y_ref[rows, :] = o.astype(y_ref.dtype)


def na_attention(u, cache_k, cache_v, layer, bias_tables):
    assert u.dtype == jnp.bfloat16
    ck = cache_k.reshape(DEC_BATCH, DEPTH, PAST_LEN, C_HEADS * HD)
    cv = cache_v.reshape(DEC_BATCH, DEPTH, PAST_LEN, C_HEADS * HD)
    cache_spec = pl.BlockSpec((None, None, PAST_LEN, HD), lambda h, b: (b, layer, 0, h))
    head = lambda col: pl.BlockSpec((DEC_SEQ, HD), lambda h, b: (LAT_BLK0 + b, col // HD + h))
    table_shape = bias_tables.shape[2:]
    return pl.pallas_call(
        _na_attn_kernel,
        grid=(C_HEADS, DEC_BATCH),
        in_specs=[head(U_QC), head(U_KC), head(U_VC), cache_spec, cache_spec,
                  pl.BlockSpec((None, None) + table_shape, lambda h, b: (layer, h, 0, 0))],
        out_specs=pl.BlockSpec((DEC_SEQ, HD), lambda h, b: (b, h)),
        out_shape=jax.ShapeDtypeStruct((N_LAT_TOK, MIX_WIDTH), jnp.bfloat16),
        compiler_params=_params(("arbitrary", "arbitrary")),
        name="na_attention",
    )(u, u, u, ck, cv, bias_tables)


def _pool_kernel(a_ref, w_ref, scale_ref, y_ref, pad_scr, *, seq):
    halo = SUBLANES
    zeros = jnp.zeros((halo, MIX_WIDTH), jnp.float32)
    pad_scr[0:halo, :] = zeros
    pad_scr[halo + seq:, :] = zeros
    pad_scr[halo:halo + seq, :] = a_ref[...].astype(jnp.float32)
    t = lax.broadcasted_iota(jnp.int32, (seq, 1), 0)
    for g, w in enumerate(POOL_WINDOWS):
        cols = slice(g * POOL_GROUP_DIM, (g + 1) * POOL_GROUP_DIM)
        lo, hi = -(w // 2), w - w // 2
        total = None
        for d in range(lo, hi):
            term = pad_scr[pl.ds(halo + d, seq), cols]
            total = term if total is None else total + term
        cnt = (jnp.minimum(t + hi, seq) - jnp.maximum(t + lo, 0)).astype(jnp.float32)
        diff = total / cnt - pad_scr[halo:halo + seq, cols]
        y = jnp.dot(diff.astype(jnp.bfloat16), w_ref[g], preferred_element_type=jnp.float32)
        y_ref[:, cols] = (y * scale_ref[:, cols]).astype(y_ref.dtype)


def pool_mix(u, w_pool, pool_scale, latent):
    seq, n_seq, blk0 = (DEC_SEQ, DEC_BATCH, LAT_BLK0) if latent else (SEQ, BATCH, 0)
    return pl.pallas_call(
        functools.partial(_pool_kernel, seq=seq),
        grid=(n_seq,),
        in_specs=[pl.BlockSpec((seq, MIX_WIDTH), lambda i: (blk0 + i, U_POOL // MIX_WIDTH)),
                  pl.BlockSpec((POOL_GROUPS, POOL_GROUP_DIM, POOL_GROUP_DIM), lambda i: (0, 0, 0)),
                  pl.BlockSpec((1, MIX_WIDTH), lambda i: (0, 0))],
        out_specs=pl.BlockSpec((seq, MIX_WIDTH), lambda i: (i, 0)),
        out_shape=jax.ShapeDtypeStruct((n_seq * seq, MIX_WIDTH), jnp.bfloat16),
        scratch_shapes=[pltpu.VMEM((seq + 2 * SUBLANES, MIX_WIDTH), jnp.float32)],
        compiler_params=_params(("arbitrary",)),
        name="pool_mix_latent" if latent else "pool_mix_ctx",
    )(u, w_pool, pool_scale)


DT_LANES = 128


def _split_bf16(x, parts):
    out = []
    rest = x
    for _ in range(parts):
        p = rest.astype(jnp.bfloat16)
        out.append(p)
        rest = rest - p.astype(jnp.float32)
    return out


def _select_rows(sel, x):
    return sum(jnp.dot(sel, p, preferred_element_type=jnp.float32) for p in _split_bf16(x, 3))


def _select_cols(x, sel):
    return sum(jnp.dot(p, sel, preferred_element_type=jnp.float32) for p in _split_bf16(x, 2))


def _ssd_kernel(xbc_ref, z_ref, dtr_ref, convw_ref, convb_ref, dtb_ref, alog_ref, dskip_ref, g_ref, *rest,
                seq, latent):
    if latent:
        h0_ref, y_ref, pad_scr, xs_scr, bc_scr, dt_scr, yacc_scr, h_scr = rest
    else:
        y_ref, hfin_ref, pad_scr, xs_scr, bc_scr, dt_scr, yacc_scr, h_scr = rest
    f32, bf16 = jnp.float32, jnp.bfloat16
    Q = SSM_CHUNK
    nc = seq // Q
    halo = SUBLANES
    gw = SSM_INNER // SSM_GROUPS
    pair_w = 2 * SSM_HEADDIM

    zeros = jnp.zeros((halo, SSM_XBC), f32)
    pad_scr[0:halo, :] = zeros
    pad_scr[halo + seq:, :] = zeros
    pad_scr[halo:halo + seq, :] = xbc_ref[...].astype(f32)
    for c in range(nc):
        acc = convb_ref[...]
        for k in range(SSM_CONV):
            start = halo + c * Q + k - SSM_CONV // 2
            acc = acc + convw_ref[k:k + 1, :] * pad_scr[start:start + Q, :]
        act = acc * jax.nn.sigmoid(acc)
        xs_scr[c * Q:(c + 1) * Q, :] = act[:, :SSM_INNER]
        bc_scr[c * Q:(c + 1) * Q, :] = act[:, SSM_INNER:]
        yacc_scr[c * Q:(c + 1) * Q, :] = act[:, :SSM_INNER] * dskip_ref[...]

    dt_lane = lax.broadcasted_iota(jnp.int32, (1, DT_LANES), 1) < 2 * SSM_HEADS
    pre = jnp.where(dt_lane, dtr_ref[...].astype(f32) + dtb_ref[...], 0.0)
    dt_scr[...] = jnp.maximum(pre, 0.0) + jnp.log1p(jnp.exp(-jnp.abs(pre)))
    a = -jnp.exp(alog_ref[...])

    if latent:
        h_scr[...] = h0_ref[...]
    else:
        h_scr[...] = jnp.zeros(h_scr.shape, f32)

    ri = lax.broadcasted_iota(jnp.int32, (Q, Q), 0)
    ci = lax.broadcasted_iota(jnp.int32, (Q, Q), 1)
    causal = (ci <= ri, ci >= ri)
    tri = tuple(m.astype(bf16) for m in causal)
    er = lax.broadcasted_iota(jnp.int32, (DT_LANES, SSM_INNER), 0)
    ec = lax.broadcasted_iota(jnp.int32, (DT_LANES, SSM_INNER), 1) // SSM_HEADDIM
    expand = tuple((er == ec + d * SSM_HEADS).astype(bf16) for d in range(2))
    lane = lax.broadcasted_iota(jnp.int32, (Q, pair_w), 1)
    first_head = lane < SSM_HEADDIM

    def step(s, carry):
        for d in range(2):
            c = s if d == 0 else nc - 1 - s
            r0 = pl.multiple_of(c * Q, Q)
            dtc = dt_scr[pl.ds(r0, Q), :]
            cum = _select_rows(tri[d], dtc * a)
            last = cum[Q - 1:Q, :] if d == 0 else cum[0:1, :]
            w_state = jnp.exp(last - cum) * dtc
            cum_t = cum.T
            dt_t = dtc.T
            w_state_x = _select_cols(w_state, expand[d])
            ecum_x = _select_cols(jnp.exp(cum), expand[d])
            decay_x = _select_cols(jnp.broadcast_to(jnp.exp(last), (SUBLANES, DT_LANES)), expand[d])[0:1, :]
            bcv = bc_scr[pl.ds(r0, Q), :]
            bmat = bcv[:, :SSM_GROUPS * SSM_STATE]
            cmat = bcv[:, SSM_GROUPS * SSM_STATE:]
            bmat_t = bmat.T
            xs_c = xs_scr[pl.ds(r0, Q), :]
            xw = (xs_c * w_state_x).astype(bf16)
            for g in range(SSM_GROUPS):
                st = slice(g * SSM_STATE, (g + 1) * SSM_STATE)
                ch = slice(g * gw, (g + 1) * gw)
                bg = bmat[:, st].astype(bf16)
                cg = cmat[:, st].astype(bf16)
                cb = lax.dot_general(cg, bg, _NT, preferred_element_type=f32)
                h_prev = h_scr[d, :, ch]
                y_inter = jnp.dot(cg, h_prev.astype(bf16), preferred_element_type=f32) * ecum_x[:, ch]
                new_state = jnp.dot(bmat_t[st, :].astype(bf16), xw[:, ch], preferred_element_type=f32)
                h_scr[d, :, ch] = h_prev * decay_x[:, ch] + new_state
                for k in range(gw // pair_w):
                    head = (g * gw + k * pair_w) // SSM_HEADDIM
                    w_pair = []
                    for ln in (d * SSM_HEADS + head, d * SSM_HEADS + head + 1):
                        col = jnp.broadcast_to(cum[:, ln:ln + 1], (Q, Q))
                        seg = jnp.exp(jnp.where(causal[d], col - cum_t[ln:ln + 1, :], NEG_INF))
                        w_pair.append(cb * seg * dt_t[ln:ln + 1, :])
                    lhs = jnp.concatenate(w_pair, axis=1).astype(bf16)
                    pc = slice(g * gw + k * pair_w, g * gw + (k + 1) * pair_w)
                    xp = xs_c[:, pc]
                    rhs = jnp.concatenate([jnp.where(first_head, xp, 0.0), jnp.where(first_head, 0.0, xp)],
                                          axis=0).astype(bf16)
                    y_pair = jnp.dot(lhs, rhs, preferred_element_type=f32) + y_inter[:, k * pair_w:(k + 1) * pair_w]
                    yacc_scr[pl.ds(r0, Q), pc] += y_pair
        return carry

    lax.fori_loop(0, nc, step, 0)

    for c in range(nc):
        rows = slice(c * Q, (c + 1) * Q)
        zc = z_ref[rows, :].astype(f32)
        y = yacc_scr[rows, :] * (zc * jax.nn.sigmoid(zc))
        ms = jnp.mean(jnp.square(y), axis=-1, keepdims=True)
        y_ref[rows, :] = (y * lax.rsqrt(ms + RMS_EPS) * g_ref[...]).astype(y_ref.dtype)
    if not latent:
        fill = jnp.zeros((HD - SSM_STATE, SSM_INNER), f32)
        for d in range(2):
            hfin_ref[d] = jnp.concatenate([h_scr[d], fill], axis=0).T[:, :SSM_STATE]


def _ssd_call(u, weights, seq, n_seq, blk0, h0=None):
    latent = h0 is not None
    state_shape = (2, SSM_STATE, SSM_INNER)
    vec = lambda n: pl.BlockSpec((1, n), lambda i: (0, 0))
    window = lambda col, width: pl.BlockSpec((pl.Element(seq), pl.Element(width)),
                                             lambda i: ((blk0 + i) * seq, col))
    in_specs = [window(U_XBC, SSM_XBC), window(U_Z, SSM_INNER),
                pl.BlockSpec((seq, DT_LANES), lambda i: (blk0 + i, U_DT // DT_LANES)),
                pl.BlockSpec((SUBLANES, SSM_XBC), lambda i: (0, 0)),
                vec(SSM_XBC), vec(DT_LANES), vec(DT_LANES), vec(SSM_INNER), vec(SSM_INNER)]
    args = [u, u, u, *weights]
    y_spec = pl.BlockSpec((seq, SSM_INNER), lambda i: (i, 0))
    y_shape = jax.ShapeDtypeStruct((n_seq * seq, SSM_INNER), jnp.bfloat16)
    if latent:
        in_specs.append(pl.BlockSpec((None,) + state_shape, lambda i: (i, 0, 0, 0)))
        args.append(h0)
        out_specs, out_shape = y_spec, y_shape
    else:
        final_shape = (2, SSM_INNER, SSM_STATE)
        out_specs = [y_spec, pl.BlockSpec((None,) + final_shape, lambda i: (i, 0, 0, 0))]
        out_shape = [y_shape, jax.ShapeDtypeStruct((n_seq,) + final_shape, jnp.float32)]
    f32 = jnp.float32
    return pl.pallas_call(
        functools.partial(_ssd_kernel, seq=seq, latent=latent),
        grid=(n_seq,),
        in_specs=in_specs,
        out_specs=out_specs,
        out_shape=out_shape,
        scratch_shapes=[pltpu.VMEM((seq + 2 * SUBLANES, SSM_XBC), f32), pltpu.VMEM((seq, SSM_INNER), f32),
                        pltpu.VMEM((seq, 2 * SSM_GROUPS * SSM_STATE), f32), pltpu.VMEM((seq, DT_LANES), f32),
                        pltpu.VMEM((seq, SSM_INNER), f32), pltpu.VMEM(state_shape, f32)],
        compiler_params=_params(("arbitrary",)),
        name="ssd_latent" if latent else "ssd_ctx",
    )(*args)


def ssd_mixer(u, conv_w, conv_b, a_log, dt_bias, d_skip, norm_g, state_l):
    f32 = jnp.float32
    lane_pad = lambda v: jnp.pad(v.reshape(1, -1).astype(f32), ((0, 0), (0, DT_LANES - v.size)))
    weights = (jnp.pad(conv_w.astype(f32), ((0, SUBLANES - SSM_CONV), (0, 0))), conv_b.reshape(1, -1),
               lane_pad(dt_bias), lane_pad(a_log), jnp.repeat(d_skip, SSM_HEADDIM).reshape(1, -1),
               norm_g.reshape(1, -1))
    h0 = state_l.transpose(0, 1, 4, 2, 3).reshape(DEC_BATCH, 2, SSM_STATE, SSM_INNER)
    yd_ctx, h_fin = _ssd_call(u, weights, SEQ, BATCH, 0)
    yd_lat = _ssd_call(u, weights, DEC_SEQ, DEC_BATCH, LAT_BLK0, h0=h0)
    return yd_ctx, yd_lat, h_fin.reshape(BATCH, 2, SSM_HEADS, SSM_HEADDIM, SSM_STATE)


def _pack_w_in(w_in_l):
    return w_in_l[:, :U_WIDTH].astype(jnp.bfloat16), w_in_l[:, U_GATES:].astype(jnp.bfloat16)


def kernel(x_prompt, x_sample, cache_b_k, cache_b_v, cache_c_k, cache_c_v, state_ssm, c, c_ctx, w_ada, b_ada, w_in, w_pool, pool_scale, attn_sink, na_rpb, conv_w, conv_b, a_log, dt_bias, d_skip, ssm_norm, w_branch, w_o, ln1_g, ln1_b, w_mlp1, w_mlp2, ln2_g, ln2_b):
    f32 = jnp.float32
    bf16 = jnp.bfloat16

    cond = jnp.concatenate([c_ctx[None, :], c, jnp.zeros((COND_PAD - N_COND, D_MODEL), f32)], axis=0)
    mod_all = ada_modulation(cond, w_ada, b_ada.reshape(DEPTH, 1, 6 * D_MODEL))
    blk_row = np.concatenate([np.zeros(N_CTX_TOK // MOD_ROWS, np.int32), 1 + np.arange(DEC_BATCH, dtype=np.int32)])
    mods = mod_all[:, blk_row].reshape(DEPTH, N_MOD_BLOCKS, 6, 1, D_MODEL).transpose(0, 2, 1, 3, 4)
    SHIFT1, SCALE1, GATE1, SHIFT2, SCALE2, GATE2 = range(6)

    x = jnp.concatenate([x_prompt.reshape(N_CTX_TOK, D_MODEL), x_sample.reshape(N_LAT_TOK, D_MODEL)], axis=0)
    h = modulate(x, mods[0], SCALE1, SHIFT1)
    rope_tables = _rope_tables()
    na_tables = _na_bias_tables(na_rpb)

    bk, bv, ckk, cvv, hs = [], [], [], [], []
    for l in range(DEPTH):
        w_u, w_gate = _pack_w_in(w_in[l])
        u = matmul(h, w_u, bf16, name="in_proj")

        pool_w = (w_pool[l].astype(bf16), pool_scale[l][None])
        ya_ctx = pool_mix(u, *pool_w, latent=False)
        ya_lat = pool_mix(u, *pool_w, latent=True)
        yb_ctx, yc_ctx = ctx_attention(u, attn_sink[l])
        yb_lat = band_attention(u, cache_b_k, cache_b_v, l, attn_sink[l], rope_tables)
        yc_lat = na_attention(u, cache_c_k, cache_c_v, l, na_tables)
        yd_ctx, yd_lat, h_ctx = ssd_mixer(u, conv_w[l], conv_b[l], a_log[l], dt_bias[l], d_skip[l], ssm_norm[l],
                                          state_ssm[:, l])

        def ctx_heads(col, heads):
            return u[:N_CTX_TOK, col:col + heads * HD].astype(f32).reshape(BATCH, SEQ, heads, HD)

        bk.append(ctx_heads(U_KB, B_KV_HEADS))
        bv.append(ctx_heads(U_VB, B_KV_HEADS))
        ckk.append(ctx_heads(U_KC, C_HEADS))
        cvv.append(ctx_heads(U_VC, C_HEADS))
        hs.append(h_ctx)

        merged = merge_branches(h, (ya_ctx, yb_ctx, yc_ctx, yd_ctx), (ya_lat, yb_lat, yc_lat, yd_lat),
                                w_gate, w_branch[l].astype(bf16))
        x, h2 = matmul_ln(merged, w_o[l].astype(bf16), x, mods[l], GATE1, ln1_g[l][None], ln1_b[l][None],
                          mods[l], SCALE2, SHIFT2, name="out_proj_ln1")
        ff = matmul(h2, w_mlp1[l].astype(bf16), bf16, relu_sq=True, name="mlp_up")
        nxt = min(l + 1, DEPTH - 1)
        x, h = matmul_ln(ff, w_mlp2[l].astype(bf16), x, mods[l], GATE2, ln2_g[l][None], ln2_b[l][None],
                         mods[nxt], SCALE1, SHIFT1, name="mlp_down_ln2")

    y_prompt = x[:N_CTX_TOK].reshape(BATCH, SEQ, D_MODEL)
    y_sample = x[N_CTX_TOK:].reshape(DEC_BATCH, DEC_SEQ, D_MODEL)
    return (y_prompt, y_sample, jnp.stack(bk, axis=1), jnp.stack(bv, axis=1), jnp.stack(ckk, axis=1),
            jnp.stack(cvv, axis=1), jnp.stack(hs, axis=1))
```

```python
import functools

import jax
import jax.numpy as jnp
import numpy as np
from jax import lax
from jax.experimental import pallas as pl
from jax.experimental.pallas import tpu as pltpu

D_MODEL = 2048
BATCH = 32
SEQ = 256
DEPTH = 4
DEC_BATCH = 8
DEC_SEQ = 1024
PAST_LEN = 256
GRID_W = 64
HD = 128
MIX_WIDTH = 1024
N_BRANCH = 4
POOL_GROUPS = 4
POOL_GROUP_DIM = MIX_WIDTH // POOL_GROUPS
POOL_WINDOWS = (2, 4, 8, 16)
B_HEADS = 8
B_KV_HEADS = 2
GQA_GROUP = B_HEADS // B_KV_HEADS
B_WINDOW = 128
B_BLOCK = 128
C_HEADS = 8
NA_KH = 8
NA_KW = 16
SSM_HEADS = 16
SSM_HEADDIM = 64
SSM_INNER = SSM_HEADS * SSM_HEADDIM
SSM_GROUPS = 2
SSM_STATE = 64
SSM_CONV = 5
SSM_CHUNK = 128
SSM_XBC = SSM_INNER + 2 * SSM_GROUPS * SSM_STATE
D_FF = 4 * D_MODEL
ROPE_BASE = 10000.0
LN_EPS = 1e-5
RMS_EPS = 1e-6
NEG_INF = -1e30
ALPHA = (2 * DEPTH) ** 0.25
ATTN_SCALE = HD ** -0.5

N_CTX_TOK = BATCH * SEQ
N_LAT_TOK = DEC_BATCH * DEC_SEQ
N_TOK = N_CTX_TOK + N_LAT_TOK
MOD_ROWS = DEC_SEQ
N_MOD_BLOCKS = N_TOK // MOD_ROWS
N_COND = 1 + DEC_BATCH
COND_PAD = 16
LAT_BLK0 = N_CTX_TOK // DEC_SEQ
GRID_ROWS = DEC_SEQ // GRID_W

U_POOL = 0
U_QB = U_POOL + MIX_WIDTH
U_KB = U_QB + B_HEADS * HD
U_VB = U_KB + B_KV_HEADS * HD
U_QC = U_VB + B_KV_HEADS * HD
U_KC = U_QC + C_HEADS * HD
U_VC = U_KC + C_HEADS * HD
U_Z = U_VC + C_HEADS * HD
U_XBC = U_Z + SSM_INNER
U_DT = U_XBC + SSM_XBC
U_GATES = U_DT + 2 * SSM_HEADS
U_WIDTH = 8192

SUBLANES = 8
VMEM_LIMIT = 56 * 1024 * 1024
_NT = (((1,), (1,)), ((), ()))


def _params(sem):
    return pltpu.CompilerParams(dimension_semantics=sem, vmem_limit_bytes=VMEM_LIMIT)


def _ada_kernel(cond_t_ref, w_ref, b_ref, o_ref):
    cond_t = cond_t_ref[...]
    s_t = (cond_t * jax.nn.sigmoid(cond_t)).astype(jnp.bfloat16)
    w = w_ref[...].astype(jnp.bfloat16)
    acc_t = lax.dot_general(w, s_t, (((0,), (0,)), ((), ())), preferred_element_type=jnp.float32)
    o_ref[...] = acc_t.T[:COND_PAD, :] + b_ref[...]


def ada_modulation(cond, w_ada, b_ada):
    bn = 2048
    n = w_ada.shape[-1]
    cond_t = jnp.pad(cond, ((0, HD - COND_PAD), (0, 0))).T
    return pl.pallas_call(
        _ada_kernel,
        grid=(DEPTH, n // bn),
        in_specs=[
            pl.BlockSpec((D_MODEL, HD), lambda l, j: (0, 0)),
            pl.BlockSpec((None, D_MODEL, bn), lambda l, j: (l, 0, j)),
            pl.BlockSpec((None, 1, bn), lambda l, j: (l, 0, j)),
        ],
        out_specs=pl.BlockSpec((None, COND_PAD, bn), lambda l, j: (l, 0, j)),
        out_shape=jax.ShapeDtypeStruct((DEPTH, COND_PAD, n), jnp.float32),
        compiler_params=_params(("arbitrary", "arbitrary")),
        name="ada_modulation",
    )(cond_t, w_ada, b_ada)


def _modulate_kernel(x_ctx_ref, x_lat_ref, scale_ref, shift_ref, x_ref, h_ref, *, n_ctx_blocks):
    def emit(src_ref):
        x = src_ref[...]
        x_ref[...] = x
        h_ref[...] = (x * (1.0 + scale_ref[...]) + shift_ref[...]).astype(h_ref.dtype)

    is_ctx = pl.program_id(0) < n_ctx_blocks
    pl.when(is_ctx)(lambda: emit(x_ctx_ref))
    pl.when(jnp.logical_not(is_ctx))(lambda: emit(x_lat_ref))


def _mod_spec(which, bm, lag=0):
    return pl.BlockSpec((None, None, 1, D_MODEL),
                        lambda i, *_: (which, (jnp.maximum(i - lag, 0) * bm) // MOD_ROWS, 0, 0))


def modulate(x_ctx, x_lat, mod, which_scale, which_shift):
    bm = 512
    n_ctx_blocks = N_CTX_TOK // bm
    row = pl.BlockSpec((bm, D_MODEL), lambda i: (i, 0))
    return pl.pallas_call(
        functools.partial(_modulate_kernel, n_ctx_blocks=n_ctx_blocks),
        grid=(N_TOK // bm,),
        in_specs=[
            pl.BlockSpec((bm, D_MODEL), lambda i: (jnp.minimum(i, n_ctx_blocks - 1), 0)),
            pl.BlockSpec((bm, D_MODEL), lambda i: (jnp.maximum(i - n_ctx_blocks, 0), 0)),
            _mod_spec(which_scale, bm),
            _mod_spec(which_shift, bm),
        ],
        out_specs=[row, row],
        out_shape=[jax.ShapeDtypeStruct((N_TOK, D_MODEL), jnp.float32),
                   jax.ShapeDtypeStruct((N_TOK, D_MODEL), jnp.bfloat16)],
        compiler_params=_params(("arbitrary",)),
        name="modulate",
    )(x_ctx, x_lat, mod, mod)


def _matmul_kernel(x_ref, w_ref, o_ref, *, relu_sq):
    acc = jnp.dot(x_ref[...], w_ref[...], preferred_element_type=jnp.float32)
    if relu_sq:
        acc = jnp.square(jnp.maximum(acc, 0.0))
    o_ref[...] = acc.astype(o_ref.dtype)


def matmul(x, w, out_dtype, relu_sq=False, bm=1024, bn=2048, name="matmul"):
    m, k = x.shape
    n = w.shape[1]
    return pl.pallas_call(
        functools.partial(_matmul_kernel, relu_sq=relu_sq),
        grid=(m // bm, n // bn),
        in_specs=[
            pl.BlockSpec((bm, k), lambda i, j: (i, 0)),
            pl.BlockSpec((k, bn), lambda i, j: (0, j)),
        ],
        out_specs=pl.BlockSpec((bm, bn), lambda i, j: (i, j)),
        out_shape=jax.ShapeDtypeStruct((m, n), out_dtype),
        compiler_params=_params(("arbitrary", "arbitrary")),
        name=name,
    )(x, w)


def _merge_kernel(h_ref, *refs, n_ctx_blocks):
    y_ctx, y_lat = refs[:N_BRANCH], refs[N_BRANCH:2 * N_BRANCH]
    wg_refs, (wb_ref, o_ref) = refs[2 * N_BRANCH:3 * N_BRANCH], refs[3 * N_BRANCH:]

    def merge(y_refs):
        h = h_ref[...]
        acc = None
        for n in range(N_BRANCH):
            gate = jax.nn.sigmoid(jnp.dot(h, wg_refs[n][...], preferred_element_type=jnp.float32))
            proj = jnp.dot(y_refs[n][...], wb_ref[n], preferred_element_type=jnp.float32)
            acc = gate * proj if acc is None else acc + gate * proj
        o_ref[...] = acc.astype(o_ref.dtype)

    is_ctx = pl.program_id(0) < n_ctx_blocks
    pl.when(is_ctx)(lambda: merge(y_ctx))
    pl.when(jnp.logical_not(is_ctx))(lambda: merge(y_lat))


def merge_branches(h, ys_ctx, ys_lat, w_gate, w_branch):
    bm, bn = 512, 512
    n_ctx_blocks = N_CTX_TOK // bm
    ctx_spec = pl.BlockSpec((bm, MIX_WIDTH), lambda i, j: (jnp.minimum(i, n_ctx_blocks - 1), 0))
    lat_spec = pl.BlockSpec((bm, MIX_WIDTH), lambda i, j: (jnp.maximum(i - n_ctx_blocks, 0), 0))
    gate_spec = lambda n: pl.BlockSpec((D_MODEL, bn), lambda i, j: (0, n * (D_MODEL // bn) + j))
    return pl.pallas_call(
        functools.partial(_merge_kernel, n_ctx_blocks=n_ctx_blocks),
        grid=(N_TOK // bm, D_MODEL // bn),
        in_specs=[
            pl.BlockSpec((bm, D_MODEL), lambda i, j: (i, 0)),
            *[ctx_spec] * N_BRANCH, *[lat_spec] * N_BRANCH,
            *[gate_spec(n) for n in range(N_BRANCH)],
            pl.BlockSpec((N_BRANCH, MIX_WIDTH, bn), lambda i, j: (0, 0, j)),
        ],
        out_specs=pl.BlockSpec((bm, bn), lambda i, j: (i, j)),
        out_shape=jax.ShapeDtypeStruct((N_TOK, D_MODEL), jnp.bfloat16),
        compiler_params=_params(("arbitrary", "arbitrary")),
        name="merge_branches",
    )(h, *ys_ctx, *ys_lat, *([w_gate] * N_BRANCH), w_branch)


def _matmul_ln_kernel(a_ref, w_ref, res_ref, gate_ref, g_ref, b_ref, scale_ref, shift_ref,
                      x_out_ref, h_out_ref, acc_even, acc_odd, *, nk):
    i = pl.program_id(0)
    k = pl.program_id(1)
    rows_per_step = acc_even.shape[0] // nk

    @pl.when((i == 0) & (k == 0))
    def _():
        acc_odd[...] = jnp.zeros(acc_odd.shape, jnp.float32)

    def step(acc_cur, acc_prev):
        part = jnp.dot(a_ref[...], w_ref[...], preferred_element_type=jnp.float32)
        if nk > 1:
            part = part + jnp.where(k > 0, acc_cur[...], 0.0)
        acc_cur[...] = part
        rows = pl.ds(pl.multiple_of(k * rows_per_step, rows_per_step), rows_per_step)
        r = ALPHA * res_ref[rows, :] + gate_ref[...] * acc_prev[rows, :]
        mu = jnp.mean(r, axis=-1, keepdims=True)
        d = r - mu
        var = jnp.mean(jnp.square(d), axis=-1, keepdims=True)
        xn = d * lax.rsqrt(var + LN_EPS) * g_ref[...] + b_ref[...]
        x_out_ref[rows, :] = xn
        h_out_ref[rows, :] = (xn * (1.0 + scale_ref[...]) + shift_ref[...]).astype(h_out_ref.dtype)

    @pl.when(i % 2 == 0)
    def _():
        step(acc_even, acc_odd)

    @pl.when(i % 2 == 1)
    def _():
        step(acc_odd, acc_even)


def matmul_ln(a, w, res, mod, which_gate, ln_g, ln_b, mod_next, which_scale, which_shift, name):
    m, kdim = a.shape
    bm, bk = 512, 2048
    nm, nk = m // bm, kdim // bk
    row = pl.BlockSpec((bm, D_MODEL), lambda i, k: (jnp.maximum(i - 1, 0), 0))
    vec = pl.BlockSpec((1, D_MODEL), lambda i, k: (0, 0))
    acc = pltpu.VMEM((bm, D_MODEL), jnp.float32)
    return pl.pallas_call(
        functools.partial(_matmul_ln_kernel, nk=nk),
        grid=(nm + 1, nk),
        in_specs=[
            pl.BlockSpec((bm, bk), lambda i, k: (jnp.minimum(i, nm - 1), k)),
            pl.BlockSpec((bk, D_MODEL), lambda i, k: (k, 0)),
            row,
            _mod_spec(which_gate, bm, lag=1),
            vec, vec,
            _mod_spec(which_scale, bm, lag=1),
            _mod_spec(which_shift, bm, lag=1),
        ],
        out_specs=[row, row],
        out_shape=[jax.ShapeDtypeStruct((m, D_MODEL), jnp.float32),
                   jax.ShapeDtypeStruct((m, D_MODEL), jnp.bfloat16)],
        scratch_shapes=[acc, acc],
        compiler_params=_params(("arbitrary", "arbitrary")),
        name=name,
    )(a, w, res, mod, ln_g, ln_b, mod_next, mod_next)


def _softmax_pv(parts, extra_logit=None):
    m = None
    for s, _ in parts:
        mi = jnp.max(s, axis=-1, keepdims=True)
        m = mi if m is None else jnp.maximum(m, mi)
    if extra_logit is not None:
        m = jnp.maximum(m, extra_logit)
    es = [jnp.exp(s - m) for s, _ in parts]
    denom = None
    for e in es:
        li = jnp.sum(e, axis=-1, keepdims=True)
        denom = li if denom is None else denom + li
    if extra_logit is not None:
        denom = denom + jnp.exp(extra_logit - m)
    inv = 1.0 / denom
    out = None
    for e, (_, v) in zip(es, parts):
        o = jnp.dot((e * inv).astype(jnp.bfloat16), v, preferred_element_type=jnp.float32)
        out = o if out is None else out + o
    return out


def _qk(q, k):
    return lax.dot_general(q, k, _NT, preferred_element_type=jnp.float32) * ATTN_SCALE


def _head(ref, h):
    return ref[:, h * HD:(h + 1) * HD]


def _ctx_attn_kernel(sink_ref, qb_ref, kb_ref, vb_ref, qc_ref, kc_ref, vc_ref, yb_ref, yc_ref):
    bf16 = jnp.bfloat16
    for h in range(B_HEADS):
        kvh = h // GQA_GROUP
        q = _head(qb_ref, h).astype(bf16)
        k = _head(kb_ref, kvh).astype(bf16)
        v = _head(vb_ref, kvh).astype(bf16)
        o = _softmax_pv([(_qk(q, k), v)], extra_logit=sink_ref[h])
        yb_ref[:, h * HD:(h + 1) * HD] = o.astype(yb_ref.dtype)
    for h in range(C_HEADS):
        q = _head(qc_ref, h).astype(bf16)
        k = _head(kc_ref, h).astype(bf16)
        v = _head(vc_ref, h).astype(bf16)
        o = _softmax_pv([(_qk(q, k), v)])
        yc_ref[:, h * HD:(h + 1) * HD] = o.astype(yc_ref.dtype)


def ctx_attention(u, sink):
    wide = lambda col: pl.BlockSpec((pl.Element(SEQ), pl.Element(MIX_WIDTH)), lambda i: (i * SEQ, col))
    kv_w = B_KV_HEADS * HD
    narrow = lambda col: pl.BlockSpec((SEQ, kv_w), lambda i: (i, col // kv_w))
    out = jax.ShapeDtypeStruct((N_CTX_TOK, MIX_WIDTH), jnp.bfloat16)
    return pl.pallas_call(
        _ctx_attn_kernel,
        grid=(BATCH,),
        in_specs=[pl.BlockSpec(memory_space=pltpu.SMEM),
                  wide(U_QB), narrow(U_KB), narrow(U_VB), wide(U_QC), wide(U_KC), wide(U_VC)],
        out_specs=[pl.BlockSpec((SEQ, MIX_WIDTH), lambda i: (i, 0))] * 2,
        out_shape=[out, out],
        compiler_params=_params(("arbitrary",)),
        name="ctx_attention",
    )(sink, u, u, u, u, u, u)


def _rope_tables():
    half = HD // 2
    quarter = half // 2
    t = np.arange(DEC_SEQ)
    freqs = ROPE_BASE ** (-jnp.arange(quarter, dtype=jnp.float32) / quarter)
    pos = jnp.stack([jnp.asarray(t // GRID_W), jnp.asarray(t % GRID_W)], axis=1).astype(jnp.float32)
    ang = pos[:, :, None] * freqs[None, None, :]
    cos = jnp.cos(ang)
    sin = jnp.sin(ang)
    zero = jnp.zeros_like(sin)
    cos_t = jnp.concatenate([cos, cos], axis=2).reshape(DEC_SEQ, HD)
    sin_a = jnp.concatenate([-sin, zero], axis=2).reshape(DEC_SEQ, HD)
    sin_b = jnp.concatenate([zero, sin], axis=2).reshape(DEC_SEQ, HD)
    return cos_t, sin_a, sin_b


def _band_attn_kernel(sink_ref, qb_ref, kb_ref, vb_ref, ck_ref, cv_ref, cos_ref, sa_ref, sb_ref,
                      y_ref, q_scr, k_scr, v_scr):
    bf16 = jnp.bfloat16
    quarter = HD // 4
    cos, sa, sb = cos_ref[...], sa_ref[...], sb_ref[...]

    def rope(x):
        return x * cos + pltpu.roll(x, HD - quarter, 1) * sa + pltpu.roll(x, quarter, 1) * sb

    for h in range(B_HEADS):
        q_scr[:, h * HD:(h + 1) * HD] = rope(_head(qb_ref, h).astype(jnp.float32)).astype(bf16)
    pad = jnp.zeros((B_WINDOW, B_KV_HEADS * HD), bf16)
    for scr in (k_scr, v_scr):
        scr[0:B_WINDOW, :] = pad
        scr[B_WINDOW + DEC_SEQ:, :] = pad
    for kvh in range(B_KV_HEADS):
        k_scr[B_WINDOW:B_WINDOW + DEC_SEQ, kvh * HD:(kvh + 1) * HD] = rope(_head(kb_ref, kvh).astype(jnp.float32)).astype(bf16)
    v_scr[B_WINDOW:B_WINDOW + DEC_SEQ, :] = vb_ref[...].astype(bf16)
    ck = ck_ref[...].astype(bf16)
    cv = cv_ref[...].astype(bf16)

    span = B_BLOCK + 2 * B_WINDOW
    rows = GQA_GROUP * B_BLOCK
    i_idx = lax.broadcasted_iota(jnp.int32, (rows, span), 0) & (B_BLOCK - 1)
    c_idx = lax.broadcasted_iota(jnp.int32, (rows, span), 1)
    rel = c_idx - i_idx
    band_ok = (rel >= 0) & (rel <= 2 * B_WINDOW)
    grp = lax.broadcasted_iota(jnp.int32, (rows, 1), 0) // B_BLOCK

    def block(n, carry):
        r0 = pl.multiple_of(n * B_BLOCK, B_BLOCK)
        kpos = c_idx + (r0 - B_WINDOW)
        valid = band_ok & (kpos >= 0) & (kpos < DEC_SEQ)
        for kvh in range(B_KV_HEADS):
            heads = [kvh * GQA_GROUP + g for g in range(GQA_GROUP)]
            q = jnp.concatenate([q_scr[pl.ds(r0, B_BLOCK), h * HD:(h + 1) * HD] for h in heads], axis=0)
            ks = k_scr[pl.ds(r0, span), kvh * HD:(kvh + 1) * HD]
            vs = v_scr[pl.ds(r0, span), kvh * HD:(kvh + 1) * HD]
            s_loc = jnp.where(valid, _qk(q, ks), NEG_INF)
            s_ctx = _qk(q, ck[:, kvh * HD:(kvh + 1) * HD])
            sink = jnp.zeros((rows, 1), jnp.float32)
            for g, h in enumerate(heads):
                sink = jnp.where(grp == g, sink_ref[h], sink)
            o = _softmax_pv([(s_loc, vs), (s_ctx, cv[:, kvh * HD:(kvh + 1) * HD])], extra_logit=sink)
            for g, h in enumerate(heads):
                y_ref[pl.ds(r0, B_BLOCK), h * HD:(h + 1) * HD] = o[g * B_BLOCK:(g + 1) * B_BLOCK].astype(y_ref.dtype)
        return carry

    lax.fori_loop(0, DEC_SEQ // B_BLOCK, block, 0)


def band_attention(u, cache_k, cache_v, layer, sink, rope_tables):
    kv_w = B_KV_HEADS * HD
    ck = cache_k.reshape(DEC_BATCH, DEPTH, PAST_LEN, kv_w)
    cv = cache_v.reshape(DEC_BATCH, DEPTH, PAST_LEN, kv_w)
    cache_spec = pl.BlockSpec((None, None, PAST_LEN, kv_w), lambda b: (b, layer, 0, 0))
    table_spec = pl.BlockSpec((DEC_SEQ, HD), lambda b: (0, 0))
    return pl.pallas_call(
        _band_attn_kernel,
        grid=(DEC_BATCH,),
        in_specs=[pl.BlockSpec(memory_space=pltpu.SMEM),
                  pl.BlockSpec((DEC_SEQ, MIX_WIDTH), lambda b: (LAT_BLK0 + b, U_QB // MIX_WIDTH)),
                  pl.BlockSpec((DEC_SEQ, kv_w), lambda b: (LAT_BLK0 + b, U_KB // kv_w)),
                  pl.BlockSpec((DEC_SEQ, kv_w), lambda b: (LAT_BLK0 + b, U_VB // kv_w)),
                  cache_spec, cache_spec, table_spec, table_spec, table_spec],
        out_specs=pl.BlockSpec((DEC_SEQ, MIX_WIDTH), lambda b: (b, 0)),
        out_shape=jax.ShapeDtypeStruct((N_LAT_TOK, MIX_WIDTH), jnp.bfloat16),
        scratch_shapes=[pltpu.VMEM((DEC_SEQ, MIX_WIDTH), jnp.bfloat16),
                        pltpu.VMEM((DEC_SEQ + 2 * B_WINDOW, kv_w), jnp.bfloat16),
                        pltpu.VMEM((DEC_SEQ + 2 * B_WINDOW, kv_w), jnp.bfloat16)],
        compiler_params=_params(("arbitrary",)),
        name="band_attention",
    )(sink, u, u, u, ck, cv, *rope_tables)


def _na_bias_tables(na_rpb):
    col = np.arange(GRID_W)
    c0 = np.clip(col - NA_KW // 2, 0, GRID_W - NA_KW)
    col_ok = (col[None, :] >= c0[:, None]) & (col[None, :] < c0[:, None] + NA_KW)
    dc_i = np.clip(col[None, :] - col[:, None], -(NA_KW - 1), NA_KW - 1) + (NA_KW - 1)
    onehot = (np.arange(2 * NA_KW - 1)[:, None, None] == dc_i[None]).astype(np.float32)
    toeplitz = jnp.einsum('lhrc,cqk->lhrqk', na_rpb.astype(jnp.float32), onehot,
                          precision=lax.Precision.HIGHEST)
    toeplitz = jnp.where(col_ok, toeplitz, NEG_INF)
    masked = jnp.full((DEPTH, C_HEADS, GRID_W, GRID_W), NEG_INF, jnp.float32)
    blocks, _ = _na_query_blocks()
    tables = {}
    for rq0, lo, hi, off in blocks:
        if off in tables:
            continue
        rows = []
        for rq in range(rq0, rq0 + NA_Q_ROWS):
            first = _na_first_key_row(rq)
            pieces = [toeplitz[:, :, rk - rq + NA_KH - 1] if first <= rk < first + NA_KH else masked
                      for rk in range(lo // GRID_W, hi // GRID_W)]
            rows.append(jnp.concatenate(pieces, axis=-1))
        tables[off] = jnp.concatenate(rows, axis=-2)
    return jnp.concatenate([tables[off] for off in sorted(tables)], axis=-1)


def _na_first_key_row(rq):
    return min(max(rq - NA_KH // 2, 0), GRID_ROWS - NA_KH)


NA_Q_ROWS = 4


def _na_query_blocks():
    blocks, offsets, width = [], {}, 0
    for rq0 in range(0, GRID_ROWS, NA_Q_ROWS):
        lo = _na_first_key_row(rq0) * GRID_W // HD * HD
        hi = -(-(_na_first_key_row(rq0 + NA_Q_ROWS - 1) + NA_KH) * GRID_W // HD) * HD
        shape = (hi - lo, rq0 * GRID_W - lo) + tuple(_na_first_key_row(rq) * GRID_W - lo
                                                     for rq in range(rq0, rq0 + NA_Q_ROWS))
        if shape not in offsets:
            offsets[shape] = width
            width += hi - lo
        blocks.append((rq0, lo, hi, offsets[shape]))
    return blocks, width


def _na_attn_kernel(q_ref, k_ref, v_ref, ck_ref, cv_ref, tab_ref, y_ref):
    bf16 = jnp.bfloat16
    ck = ck_ref[...].astype(bf16)
    cv = cv_ref[...].astype(bf16)
    for rq0, lo, hi, off in _na_query_blocks()[0]:
        rows = slice(rq0 * GRID_W, (rq0 + NA_Q_ROWS) * GRID_W)
        q = q_ref[rows, :]
        s_loc = _qk(q, k_ref[lo:hi, :]) + tab_ref[:, off:off + hi - lo]
        o = _softmax_pv([(s_loc, v_ref[lo:hi, :]), (_qk(q, ck), cv)])
        y_ref[rows, :] = o.astype(y_ref.dtype)


def na_attention(u, cache_k, cache_v, layer, bias_tables):
    assert u.dtype == jnp.bfloat16
    ck = cache_k.reshape(DEC_BATCH, DEPTH, PAST_LEN, C_HEADS * HD)
    cv = cache_v.reshape(DEC_BATCH, DEPTH, PAST_LEN, C_HEADS * HD)
    cache_spec = pl.BlockSpec((None, None, PAST_LEN, HD), lambda h, b: (b, layer, 0, h))
    head = lambda col: pl.BlockSpec((DEC_SEQ, HD), lambda h, b: (LAT_BLK0 + b, col // HD + h))
    table_shape = bias_tables.shape[2:]
    return pl.pallas_call(
        _na_attn_kernel,
        grid=(C_HEADS, DEC_BATCH),
        in_specs=[head(U_QC), head(U_KC), head(U_VC), cache_spec, cache_spec,
                  pl.BlockSpec((None, None) + table_shape, lambda h, b: (layer, h, 0, 0))],
        out_specs=pl.BlockSpec((DEC_SEQ, HD), lambda h, b: (b, h)),
        out_shape=jax.ShapeDtypeStruct((N_LAT_TOK, MIX_WIDTH), jnp.bfloat16),
        compiler_params=_params(("arbitrary", "arbitrary")),
        name="na_attention",
    )(u, u, u, ck, cv, bias_tables)


def _pool_kernel(a_ref, w_ref, scale_ref, y_ref, pad_scr, *, seq):
    halo = SUBLANES
    zeros = jnp.zeros((halo, MIX_WIDTH), jnp.float32)
    pad_scr[0:halo, :] = zeros
    pad_scr[halo + seq:, :] = zeros
    pad_scr[halo:halo + seq, :] = a_ref[...].astype(jnp.float32)
    t = lax.broadcasted_iota(jnp.int32, (seq, 1), 0)
    for g, w in enumerate(POOL_WINDOWS):
        cols = slice(g * POOL_GROUP_DIM, (g + 1) * POOL_GROUP_DIM)
        lo, hi = -(w // 2), w - w // 2
        total = None
        for d in range(lo, hi):
            term = pad_scr[pl.ds(halo + d, seq), cols]
            total = term if total is None else total + term
        cnt = (jnp.minimum(t + hi, seq) - jnp.maximum(t + lo, 0)).astype(jnp.float32)
        diff = total / cnt - pad_scr[halo:halo + seq, cols]
        y = jnp.dot(diff.astype(jnp.bfloat16), w_ref[g], preferred_element_type=jnp.float32)
        y_ref[:, cols] = (y * scale_ref[:, cols]).astype(y_ref.dtype)


def pool_mix(u, w_pool, pool_scale, latent):
    seq, n_seq, blk0 = (DEC_SEQ, DEC_BATCH, LAT_BLK0) if latent else (SEQ, BATCH, 0)
    return pl.pallas_call(
        functools.partial(_pool_kernel, seq=seq),
        grid=(n_seq,),
        in_specs=[pl.BlockSpec((seq, MIX_WIDTH), lambda i: (blk0 + i, U_POOL // MIX_WIDTH)),
                  pl.BlockSpec((POOL_GROUPS, POOL_GROUP_DIM, POOL_GROUP_DIM), lambda i: (0, 0, 0)),
                  pl.BlockSpec((1, MIX_WIDTH), lambda i: (0, 0))],
        out_specs=pl.BlockSpec((seq, MIX_WIDTH), lambda i: (i, 0)),
        out_shape=jax.ShapeDtypeStruct((n_seq * seq, MIX_WIDTH), jnp.bfloat16),
        scratch_shapes=[pltpu.VMEM((seq + 2 * SUBLANES, MIX_WIDTH), jnp.float32)],
        compiler_params=_params(("arbitrary",)),
        name="pool_mix_latent" if latent else "pool_mix_ctx",
    )(u, w_pool, pool_scale)


DT_LANES = 128


def _split_bf16(x, parts):
    out = []
    rest = x
    for _ in range(parts):
        p = rest.astype(jnp.bfloat16)
        out.append(p)
        rest = rest - p.astype(jnp.float32)
    return out


def _select_rows(sel, x):
    return sum(jnp.dot(sel, p, preferred_element_type=jnp.float32) for p in _split_bf16(x, 3))


def _select_cols(x, sel):
    return sum(jnp.dot(p, sel, preferred_element_type=jnp.float32) for p in _split_bf16(x, 2))


def _ssd_kernel(xbc_ref, z_ref, dtr_ref, convw_ref, convb_ref, dtb_ref, alog_ref, dskip_ref, g_ref, *rest,
                seq, latent):
    if latent:
        h0_ref, y_ref, pad_scr, xs_scr, bc_scr, dt_scr, yacc_scr, h_scr = rest
    else:
        y_ref, hfin_ref, pad_scr, xs_scr, bc_scr, dt_scr, yacc_scr, h_scr = rest
    f32, bf16 = jnp.float32, jnp.bfloat16
    Q = SSM_CHUNK
    nc = seq // Q
    halo = SUBLANES
    gw = SSM_INNER // SSM_GROUPS
    pair_w = 2 * SSM_HEADDIM

    zeros = jnp.zeros((halo, SSM_XBC), f32)
    pad_scr[0:halo, :] = zeros
    pad_scr[halo + seq:, :] = zeros
    pad_scr[halo:halo + seq, :] = xbc_ref[...].astype(f32)
    for c in range(nc):
        acc = convb_ref[...]
        for k in range(SSM_CONV):
            start = halo + c * Q + k - SSM_CONV // 2
            acc = acc + convw_ref[k:k + 1, :] * pad_scr[start:start + Q, :]
        act = acc * jax.nn.sigmoid(acc)
        xs_scr[c * Q:(c + 1) * Q, :] = act[:, :SSM_INNER]
        bc_scr[c * Q:(c + 1) * Q, :] = act[:, SSM_INNER:]
        yacc_scr[c * Q:(c + 1) * Q, :] = act[:, :SSM_INNER] * dskip_ref[...]

    dt_lane = lax.broadcasted_iota(jnp.int32, (1, DT_LANES), 1) < 2 * SSM_HEADS
    pre = jnp.where(dt_lane, dtr_ref[...].astype(f32) + dtb_ref[...], 0.0)
    dt_scr[...] = jnp.maximum(pre, 0.0) + jnp.log1p(jnp.exp(-jnp.abs(pre)))
    a = -jnp.exp(alog_ref[...])

    if latent:
        h_scr[...] = h0_ref[...]
    else:
        h_scr[...] = jnp.zeros(h_scr.shape, f32)

    ri = lax.broadcasted_iota(jnp.int32, (Q, Q), 0)
    ci = lax.broadcasted_iota(jnp.int32, (Q, Q), 1)
    causal = (ci <= ri, ci >= ri)
    tri = tuple(m.astype(bf16) for m in causal)
    er = lax.broadcasted_iota(jnp.int32, (DT_LANES, SSM_INNER), 0)
    ec = lax.broadcasted_iota(jnp.int32, (DT_LANES, SSM_INNER), 1) // SSM_HEADDIM
    expand = tuple((er == ec + d * SSM_HEADS).astype(bf16) for d in range(2))
    lane = lax.broadcasted_iota(jnp.int32, (Q, pair_w), 1)
    first_head = lane < SSM_HEADDIM

    def step(s, carry):
        for d in range(2):
            c = s if d == 0 else nc - 1 - s
            r0 = pl.multiple_of(c * Q, Q)
            dtc = dt_scr[pl.ds(r0, Q), :]
            cum = _select_rows(tri[d], dtc * a)
            last = cum[Q - 1:Q, :] if d == 0 else cum[0:1, :]
            w_state = jnp.exp(last - cum) * dtc
            cum_t = cum.T
            dt_t = dtc.T
            w_state_x = _select_cols(w_state, expand[d])
            ecum_x = _select_cols(jnp.exp(cum), expand[d])
            decay_x = _select_cols(jnp.broadcast_to(jnp.exp(last), (SUBLANES, DT_LANES)), expand[d])[0:1, :]
            bcv = bc_scr[pl.ds(r0, Q), :]
            bmat = bcv[:, :SSM_GROUPS * SSM_STATE]
            cmat = bcv[:, SSM_GROUPS * SSM_STATE:]
            bmat_t = bmat.T
            xs_c = xs_scr[pl.ds(r0, Q), :]
            xw = (xs_c * w_state_x).astype(bf16)
            for g in range(SSM_GROUPS):
                st = slice(g * SSM_STATE, (g + 1) * SSM_STATE)
                ch = slice(g * gw, (g + 1) * gw)
                bg = bmat[:, st].astype(bf16)
                cg = cmat[:, st].astype(bf16)
                cb = lax.dot_general(cg, bg, _NT, preferred_element_type=f32)
                h_prev = h_scr[d, :, ch]
                y_inter = jnp.dot(cg, h_prev.astype(bf16), preferred_element_type=f32) * ecum_x[:, ch]
                new_state = jnp.dot(bmat_t[st, :].astype(bf16), xw[:, ch], preferred_element_type=f32)
                h_scr[d, :, ch] = h_prev * decay_x[:, ch] + new_state
                for k in range(gw // pair_w):
                    head = (g * gw + k * pair_w) // SSM_HEADDIM
                    w_pair = []
                    for ln in (d * SSM_HEADS + head, d * SSM_HEADS + head + 1):
                        col = jnp.broadcast_to(cum[:, ln:ln + 1], (Q, Q))
                        seg = jnp.exp(jnp.where(causal[d], col - cum_t[ln:ln + 1, :], NEG_INF))
                        w_pair.append(cb * seg * dt_t[ln:ln + 1, :])
                    lhs = jnp.concatenate(w_pair, axis=1).astype(bf16)
                    pc = slice(g * gw + k * pair_w, g * gw + (k + 1) * pair_w)
                    xp = xs_c[:, pc]
                    rhs = jnp.concatenate([jnp.where(first_head, xp, 0.0), jnp.where(first_head, 0.0, xp)],
                                          axis=0).astype(bf16)
                    y_pair = jnp.dot(lhs, rhs, preferred_element_type=f32) + y_inter[:, k * pair_w:(k + 1) * pair_w]
                    yacc_scr[pl.ds(r0, Q), pc] += y_pair
        return carry

    lax.fori_loop(0, nc, step, 0)

    for c in range(nc):
        rows = slice(c * Q, (c + 1) * Q)
        zc = z_ref[rows, :].astype(f32)
        y = yacc_scr[rows, :] * (zc * jax.nn.sigmoid(zc))
        ms = jnp.mean(jnp.square(y), axis=-1, keepdims=True)
        y_ref[rows, :] = (y * lax.rsqrt(ms + RMS_EPS) * g_ref[...]).astype(y_ref.dtype)
    if not latent:
        fill = jnp.zeros((HD - SSM_STATE, SSM_INNER), f32)
        for d in range(2):
            hfin_ref[d] = jnp.concatenate([h_scr[d], fill], axis=0).T[:, :SSM_STATE]


def _ssd_call(u, weights, seq, n_seq, blk0, h0=None):
    latent = h0 is not None
    state_shape = (2, SSM_STATE, SSM_INNER)
    vec = lambda n: pl.BlockSpec((1, n), lambda i: (0, 0))
    window = lambda col, width: pl.BlockSpec((pl.Element(seq), pl.Element(width)),
                                             lambda i: ((blk0 + i) * seq, col))
    in_specs = [window(U_XBC, SSM_XBC), window(U_Z, SSM_INNER),
                pl.BlockSpec((seq, DT_LANES), lambda i: (blk0 + i, U_DT // DT_LANES)),
                pl.BlockSpec((SUBLANES, SSM_XBC), lambda i: (0, 0)),
                vec(SSM_XBC), vec(DT_LANES), vec(DT_LANES), vec(SSM_INNER), vec(SSM_INNER)]
    args = [u, u, u, *weights]
    y_spec = pl.BlockSpec((seq, SSM_INNER), lambda i: (i, 0))
    y_shape = jax.ShapeDtypeStruct((n_seq * seq, SSM_INNER), jnp.bfloat16)
    if latent:
        in_specs.append(pl.BlockSpec((None,) + state_shape, lambda i: (i, 0, 0, 0)))
        args.append(h0)
        out_specs, out_shape = y_spec, y_shape
    else:
        final_shape = (2, SSM_INNER, SSM_STATE)
        out_specs = [y_spec, pl.BlockSpec((None,) + final_shape, lambda i: (i, 0, 0, 0))]
        out_shape = [y_shape, jax.ShapeDtypeStruct((n_seq,) + final_shape, jnp.float32)]
    f32 = jnp.float32
    return pl.pallas_call(
        functools.partial(_ssd_kernel, seq=seq, latent=latent),
        grid=(n_seq,),
        in_specs=in_specs,
        out_specs=out_specs,
        out_shape=out_shape,
        scratch_shapes=[pltpu.VMEM((seq + 2 * SUBLANES, SSM_XBC), f32), pltpu.VMEM((seq, SSM_INNER), f32),
                        pltpu.VMEM((seq, 2 * SSM_GROUPS * SSM_STATE), f32), pltpu.VMEM((seq, DT_LANES), f32),
                        pltpu.VMEM((seq, SSM_INNER), f32), pltpu.VMEM(state_shape, f32)],
        compiler_params=_params(("arbitrary",)),
        name="ssd_latent" if latent else "ssd_ctx",
    )(*args)


def ssd_mixer(u, conv_w, conv_b, a_log, dt_bias, d_skip, norm_g, state_l):
    f32 = jnp.float32
    lane_pad = lambda v: jnp.pad(v.reshape(1, -1).astype(f32), ((0, 0), (0, DT_LANES - v.size)))
    weights = (jnp.pad(conv_w.astype(f32), ((0, SUBLANES - SSM_CONV), (0, 0))), conv_b.reshape(1, -1),
               lane_pad(dt_bias), lane_pad(a_log), jnp.repeat(d_skip, SSM_HEADDIM).reshape(1, -1),
               norm_g.reshape(1, -1))
    h0 = state_l.transpose(0, 1, 4, 2, 3).reshape(DEC_BATCH, 2, SSM_STATE, SSM_INNER)
    yd_ctx, h_fin = _ssd_call(u, weights, SEQ, BATCH, 0)
    yd_lat = _ssd_call(u, weights, DEC_SEQ, DEC_BATCH, LAT_BLK0, h0=h0)
    return yd_ctx, yd_lat, h_fin.reshape(BATCH, 2, SSM_HEADS, SSM_HEADDIM, SSM_STATE)


def _pack_w_in(w_in_l):
    return w_in_l[:, :U_WIDTH].astype(jnp.bfloat16), w_in_l[:, U_GATES:].astype(jnp.bfloat16)


def kernel(x_prompt, x_sample, cache_b_k, cache_b_v, cache_c_k, cache_c_v, state_ssm, c, c_ctx, w_ada, b_ada, w_in, w_pool, pool_scale, attn_sink, na_rpb, conv_w, conv_b, a_log, dt_bias, d_skip, ssm_norm, w_branch, w_o, ln1_g, ln1_b, w_mlp1, w_mlp2, ln2_g, ln2_b):
    f32 = jnp.float32
    bf16 = jnp.bfloat16

    cond = jnp.concatenate([c_ctx[None, :], c, jnp.zeros((COND_PAD - N_COND, D_MODEL), f32)], axis=0)
    mod_all = ada_modulation(cond, w_ada, b_ada.reshape(DEPTH, 1, 6 * D_MODEL))
    blk_row = np.concatenate([np.zeros(N_CTX_TOK // MOD_ROWS, np.int32), 1 + np.arange(DEC_BATCH, dtype=np.int32)])
    mods = mod_all[:, blk_row].reshape(DEPTH, N_MOD_BLOCKS, 6, 1, D_MODEL).transpose(0, 2, 1, 3, 4)
    SHIFT1, SCALE1, GATE1, SHIFT2, SCALE2, GATE2 = range(6)

    x, h = modulate(x_prompt.reshape(N_CTX_TOK, D_MODEL), x_sample.reshape(N_LAT_TOK, D_MODEL),
                    mods[0], SCALE1, SHIFT1)
    rope_tables = _rope_tables()
    na_tables = _na_bias_tables(na_rpb)

    bk, bv, ckk, cvv, hs = [], [], [], [], []
    for l in range(DEPTH):
        w_u, w_gate = _pack_w_in(w_in[l])
        u = matmul(h, w_u, bf16, name="in_proj")

        pool_w = (w_pool[l].astype(bf16), pool_scale[l][None])
        ya_ctx = pool_mix(u, *pool_w, latent=False)
        ya_lat = pool_mix(u, *pool_w, latent=True)
        yb_ctx, yc_ctx = ctx_attention(u, attn_sink[l])
        yb_lat = band_attention(u, cache_b_k, cache_b_v, l, attn_sink[l], rope_tables)
        yc_lat = na_attention(u, cache_c_k, cache_c_v, l, na_tables)
        yd_ctx, yd_lat, h_ctx = ssd_mixer(u, conv_w[l], conv_b[l], a_log[l], dt_bias[l], d_skip[l], ssm_norm[l],
                                          state_ssm[:, l])

        def ctx_heads(col, heads):
            return u[:N_CTX_TOK, col:col + heads * HD].astype(f32).reshape(BATCH, SEQ, heads, HD)

        bk.append(ctx_heads(U_KB, B_KV_HEADS))
        bv.append(ctx_heads(U_VB, B_KV_HEADS))
        ckk.append(ctx_heads(U_KC, C_HEADS))
        cvv.append(ctx_heads(U_VC, C_HEADS))
        hs.append(h_ctx)

        merged = merge_branches(h, (ya_ctx, yb_ctx, yc_ctx, yd_ctx), (ya_lat, yb_lat, yc_lat, yd_lat),
                                w_gate, w_branch[l].astype(bf16))
        x, h2 = matmul_ln(merged, w_o[l].astype(bf16), x, mods[l], GATE1, ln1_g[l][None], ln1_b[l][None],
                          mods[l], SCALE2, SHIFT2, name="out_proj_ln1")
        ff = matmul(h2, w_mlp1[l].astype(bf16), bf16, relu_sq=True, name="mlp_up")
        nxt = min(l + 1, DEPTH - 1)
        x, h = matmul_ln(ff, w_mlp2[l].astype(bf16), x, mods[l], GATE2, ln2_g[l][None], ln2_b[l][None],
                         mods[nxt], SCALE1, SHIFT1, name="mlp_down_ln2")

    y_prompt = x[:N_CTX_TOK].reshape(BATCH, SEQ, D_MODEL)
    y_sample = x[N_CTX_TOK:].reshape(DEC_BATCH, DEC_SEQ, D_MODEL)
    return (y_prompt, y_sample, jnp.stack(bk, axis=1), jnp.stack(bv, axis=1), jnp.stack(ckk, axis=1),
            jnp.stack(cvv, axis=1), jnp.stack(hs, axis=1))
```

```python
import functools

import jax
import jax.numpy as jnp
import numpy as np
from jax import lax
from jax.experimental import pallas as pl
from jax.experimental.pallas import tpu as pltpu

D_MODEL = 2048
BATCH = 32
SEQ = 256
DEPTH = 4
DEC_BATCH = 8
DEC_SEQ = 1024
PAST_LEN = 256
GRID_W = 64
HD = 128
MIX_WIDTH = 1024
N_BRANCH = 4
POOL_GROUPS = 4
POOL_GROUP_DIM = MIX_WIDTH // POOL_GROUPS
POOL_WINDOWS = (2, 4, 8, 16)
B_HEADS = 8
B_KV_HEADS = 2
GQA_GROUP = B_HEADS // B_KV_HEADS
B_WINDOW = 128
B_BLOCK = 128
C_HEADS = 8
NA_KH = 8
NA_KW = 16
SSM_HEADS = 16
SSM_HEADDIM = 64
SSM_INNER = SSM_HEADS * SSM_HEADDIM
SSM_GROUPS = 2
SSM_STATE = 64
SSM_CONV = 5
SSM_CHUNK = 128
SSM_XBC = SSM_INNER + 2 * SSM_GROUPS * SSM_STATE
D_FF = 4 * D_MODEL
ROPE_BASE = 10000.0
LN_EPS = 1e-5
RMS_EPS = 1e-6
NEG_INF = -1e30
ALPHA = (2 * DEPTH) ** 0.25
ATTN_SCALE = HD ** -0.5

N_CTX_TOK = BATCH * SEQ
N_LAT_TOK = DEC_BATCH * DEC_SEQ
N_TOK = N_CTX_TOK + N_LAT_TOK
MOD_ROWS = DEC_SEQ
N_MOD_BLOCKS = N_TOK // MOD_ROWS
N_COND = 1 + DEC_BATCH
COND_PAD = 16
LAT_BLK0 = N_CTX_TOK // DEC_SEQ
GRID_ROWS = DEC_SEQ // GRID_W

U_POOL = 0
U_QB = U_POOL + MIX_WIDTH
U_KB = U_QB + B_HEADS * HD
U_VB = U_KB + B_KV_HEADS * HD
U_QC = U_VB + B_KV_HEADS * HD
U_KC = U_QC + C_HEADS * HD
U_VC = U_KC + C_HEADS * HD
U_Z = U_VC + C_HEADS * HD
U_XBC = U_Z + SSM_INNER
U_DT = U_XBC + SSM_XBC
U_GATES = U_DT + 2 * SSM_HEADS
U_WIDTH = 8192

SUBLANES = 8
VMEM_LIMIT = 56 * 1024 * 1024
_NT = (((1,), (1,)), ((), ()))


def _params(sem):
    return pltpu.CompilerParams(dimension_semantics=sem, vmem_limit_bytes=VMEM_LIMIT)


def _ada_kernel(cond_t_ref, w_ref, b_ref, o_ref):
    cond_t = cond_t_ref[...]
    s_t = (cond_t * jax.nn.sigmoid(cond_t)).astype(jnp.bfloat16)
    w = w_ref[...].astype(jnp.bfloat16)
    acc_t = lax.dot_general(w, s_t, (((0,), (0,)), ((), ())), preferred_element_type=jnp.float32)
    o_ref[...] = acc_t.T[:COND_PAD, :] + b_ref[...]


def ada_modulation(cond, w_ada, b_ada):
    bn = 2048
    n = w_ada.shape[-1]
    cond_t = jnp.pad(cond, ((0, HD - COND_PAD), (0, 0))).T
    return pl.pallas_call(
        _ada_kernel,
        grid=(DEPTH, n // bn),
        in_specs=[
            pl.BlockSpec((D_MODEL, HD), lambda l, j: (0, 0)),
            pl.BlockSpec((None, D_MODEL, bn), lambda l, j: (l, 0, j)),
            pl.BlockSpec((None, 1, bn), lambda l, j: (l, 0, j)),
        ],
        out_specs=pl.BlockSpec((None, COND_PAD, bn), lambda l, j: (l, 0, j)),
        out_shape=jax.ShapeDtypeStruct((DEPTH, COND_PAD, n), jnp.float32),
        compiler_params=_params(("arbitrary", "arbitrary")),
        name="ada_modulation",
    )(cond_t, w_ada, b_ada)


def _modulate_kernel(x_ctx_ref, x_lat_ref, scale_ref, shift_ref, x_ref, h_ref, *, n_ctx_blocks):
    def emit(src_ref):
        x = src_ref[...]
        x_ref[...] = x
        h_ref[...] = (x * (1.0 + scale_ref[...]) + shift_ref[...]).astype(h_ref.dtype)

    is_ctx = pl.program_id(0) < n_ctx_blocks
    pl.when(is_ctx)(lambda: emit(x_ctx_ref))
    pl.when(jnp.logical_not(is_ctx))(lambda: emit(x_lat_ref))


def _mod_spec(which, bm, lag=0):
    return pl.BlockSpec((None, None, 1, D_MODEL),
                        lambda i, *_: (which, (jnp.maximum(i - lag, 0) * bm) // MOD_ROWS, 0, 0))


def modulate(x_ctx, x_lat, mod, which_scale, which_shift):
    bm = 512
    n_ctx_blocks = N_CTX_TOK // bm
    row = pl.BlockSpec((bm, D_MODEL), lambda i: (i, 0))
    return pl.pallas_call(
        functools.partial(_modulate_kernel, n_ctx_blocks=n_ctx_blocks),
        grid=(N_TOK // bm,),
        in_specs=[
            pl.BlockSpec((bm, D_MODEL), lambda i: (jnp.minimum(i, n_ctx_blocks - 1), 0)),
            pl.BlockSpec((bm, D_MODEL), lambda i: (jnp.maximum(i - n_ctx_blocks, 0), 0)),
            _mod_spec(which_scale, bm),
            _mod_spec(which_shift, bm),
        ],
        out_specs=[row, row],
        out_shape=[jax.ShapeDtypeStruct((N_TOK, D_MODEL), jnp.float32),
                   jax.ShapeDtypeStruct((N_TOK, D_MODEL), jnp.bfloat16)],
        compiler_params=_params(("arbitrary",)),
        name="modulate",
    )(x_ctx, x_lat, mod, mod)


def _matmul_kernel(x_ref, w_ref, o_ref, *, relu_sq):
    acc = jnp.dot(x_ref[...], w_ref[...], preferred_element_type=jnp.float32)
    if relu_sq:
        acc = jnp.square(jnp.maximum(acc, 0.0))
    o_ref[...] = acc.astype(o_ref.dtype)


def matmul(x, w, out_dtype, relu_sq=False, bm=1024, bn=2048, name="matmul"):
    m, k = x.shape
    n = w.shape[1]
    return pl.pallas_call(
        functools.partial(_matmul_kernel, relu_sq=relu_sq),
        grid=(m // bm, n // bn),
        in_specs=[
            pl.BlockSpec((bm, k), lambda i, j: (i, 0)),
            pl.BlockSpec((k, bn), lambda i, j: (0, j)),
        ],
        out_specs=pl.BlockSpec((bm, bn), lambda i, j: (i, j)),
        out_shape=jax.ShapeDtypeStruct((m, n), out_dtype),
        compiler_params=_params(("arbitrary", "arbitrary")),
        name=name,
    )(x, w)


def _merge_kernel(h_ref, *refs, n_ctx_blocks):
    y_ctx, y_lat = refs[:N_BRANCH], refs[N_BRANCH:2 * N_BRANCH]
    wg_refs, (wb_ref, o_ref) = refs[2 * N_BRANCH:3 * N_BRANCH], refs[3 * N_BRANCH:]

    def merge(y_refs):
        h = h_ref[...]
        acc = None
        for n in range(N_BRANCH):
            gate = jax.nn.sigmoid(jnp.dot(h, wg_refs[n][...], preferred_element_type=jnp.float32))
            proj = jnp.dot(y_refs[n][...], wb_ref[n], preferred_element_type=jnp.float32)
            acc = gate * proj if acc is None else acc + gate * proj
        o_ref[...] = acc.astype(o_ref.dtype)

    is_ctx = pl.program_id(0) < n_ctx_blocks
    pl.when(is_ctx)(lambda: merge(y_ctx))
    pl.when(jnp.logical_not(is_ctx))(lambda: merge(y_lat))


def merge_branches(h, ys_ctx, ys_lat, w_gate, w_branch):
    bm, bn = 512, 512
    n_ctx_blocks = N_CTX_TOK // bm
    ctx_spec = pl.BlockSpec((bm, MIX_WIDTH), lambda i, j: (jnp.minimum(i, n_ctx_blocks - 1), 0))
    lat_spec = pl.BlockSpec((bm, MIX_WIDTH), lambda i, j: (jnp.maximum(i - n_ctx_blocks, 0), 0))
    gate_spec = lambda n: pl.BlockSpec((D_MODEL, bn), lambda i, j: (0, n * (D_MODEL // bn) + j))
    return pl.pallas_call(
        functools.partial(_merge_kernel, n_ctx_blocks=n_ctx_blocks),
        grid=(N_TOK // bm, D_MODEL // bn),
        in_specs=[
            pl.BlockSpec((bm, D_MODEL), lambda i, j: (i, 0)),
            *[ctx_spec] * N_BRANCH, *[lat_spec] * N_BRANCH,
            *[gate_spec(n) for n in range(N_BRANCH)],
            pl.BlockSpec((N_BRANCH, MIX_WIDTH, bn), lambda i, j: (0, 0, j)),
        ],
        out_specs=pl.BlockSpec((bm, bn), lambda i, j: (i, j)),
        out_shape=jax.ShapeDtypeStruct((N_TOK, D_MODEL), jnp.bfloat16),
        compiler_params=_params(("arbitrary", "arbitrary")),
        name="merge_branches",
    )(h, *ys_ctx, *ys_lat, *([w_gate] * N_BRANCH), w_branch)


def _matmul_ln_kernel(a_ref, w_ref, res_ref, gate_ref, g_ref, b_ref, scale_ref, shift_ref,
                      x_out_ref, h_out_ref, acc_even, acc_odd, *, nk):
    i = pl.program_id(0)
    k = pl.program_id(1)
    rows_per_step = acc_even.shape[0] // nk

    @pl.when((i == 0) & (k == 0))
    def _():
        acc_odd[...] = jnp.zeros(acc_odd.shape, jnp.float32)

    def step(acc_cur, acc_prev):
        part = jnp.dot(a_ref[...], w_ref[...], preferred_element_type=jnp.float32)
        if nk > 1:
            part = part + jnp.where(k > 0, acc_cur[...], 0.0)
        acc_cur[...] = part
        rows = pl.ds(pl.multiple_of(k * rows_per_step, rows_per_step), rows_per_step)
        r = ALPHA * res_ref[...] + gate_ref[...] * acc_prev[rows, :]
        mu = jnp.mean(r, axis=-1, keepdims=True)
        d = r - mu
        var = jnp.mean(jnp.square(d), axis=-1, keepdims=True)
        xn = d * lax.rsqrt(var + LN_EPS) * g_ref[...] + b_ref[...]
        x_out_ref[...] = xn
        h_out_ref[...] = (xn * (1.0 + scale_ref[...]) + shift_ref[...]).astype(h_out_ref.dtype)

    @pl.when(i % 2 == 0)
    def _():
        step(acc_even, acc_odd)

    @pl.when(i % 2 == 1)
    def _():
        step(acc_odd, acc_even)


def matmul_ln(a, w, res, mod, which_gate, ln_g, ln_b, mod_next, which_scale, which_shift, name):
    m, kdim = a.shape
    bm, bk = 1024, 1024
    nm, nk = m // bm, kdim // bk
    row = pl.BlockSpec((bm // nk, D_MODEL), lambda i, k: (jnp.where(i > 0, (i - 1) * nk + k, 0), 0))
    vec = pl.BlockSpec((1, D_MODEL), lambda i, k: (0, 0))
    acc = pltpu.VMEM((bm, D_MODEL), jnp.float32)
    return pl.pallas_call(
        functools.partial(_matmul_ln_kernel, nk=nk),
        grid=(nm + 1, nk),
        in_specs=[
            pl.BlockSpec((bm, bk), lambda i, k: (jnp.minimum(i, nm - 1), k)),
            pl.BlockSpec((bk, D_MODEL), lambda i, k: (k, 0)),
            row,
            _mod_spec(which_gate, bm, lag=1),
            vec, vec,
            _mod_spec(which_scale, bm, lag=1),
            _mod_spec(which_shift, bm, lag=1),
        ],
        out_specs=[row, row],
        out_shape=[jax.ShapeDtypeStruct((m, D_MODEL), jnp.float32),
                   jax.ShapeDtypeStruct((m, D_MODEL), jnp.bfloat16)],
        scratch_shapes=[acc, acc],
        compiler_params=_params(("arbitrary", "arbitrary")),
        name=name,
    )(a, w, res, mod, ln_g, ln_b, mod_next, mod_next)


def _softmax_pv(parts, extra_logit=None):
    m = None
    for s, _ in parts:
        mi = jnp.max(s, axis=-1, keepdims=True)
        m = mi if m is None else jnp.maximum(m, mi)
    if extra_logit is not None:
        m = jnp.maximum(m, extra_logit)
    es = [jnp.exp(s - m) for s, _ in parts]
    denom = None
    for e in es:
        li = jnp.sum(e, axis=-1, keepdims=True)
        denom = li if denom is None else denom + li
    if extra_logit is not None:
        denom = denom + jnp.exp(extra_logit - m)
    inv = 1.0 / denom
    out = None
    for e, (_, v) in zip(es, parts):
        o = jnp.dot((e * inv).astype(jnp.bfloat16), v, preferred_element_type=jnp.float32)
        out = o if out is None else out + o
    return out


def _qk(q, k):
    return lax.dot_general(q, k, _NT, preferred_element_type=jnp.float32) * ATTN_SCALE


def _head(ref, h):
    return ref[:, h * HD:(h + 1) * HD]


def _ctx_attn_kernel(sink_ref, qb_ref, kb_ref, vb_ref, qc_ref, kc_ref, vc_ref, yb_ref, yc_ref):
    bf16 = jnp.bfloat16
    for h in range(B_HEADS):
        kvh = h // GQA_GROUP
        q = _head(qb_ref, h).astype(bf16)
        k = _head(kb_ref, kvh).astype(bf16)
        v = _head(vb_ref, kvh).astype(bf16)
        o = _softmax_pv([(_qk(q, k), v)], extra_logit=sink_ref[h])
        yb_ref[:, h * HD:(h + 1) * HD] = o.astype(yb_ref.dtype)
    for h in range(C_HEADS):
        q = _head(qc_ref, h).astype(bf16)
        k = _head(kc_ref, h).astype(bf16)
        v = _head(vc_ref, h).astype(bf16)
        o = _softmax_pv([(_qk(q, k), v)])
        yc_ref[:, h * HD:(h + 1) * HD] = o.astype(yc_ref.dtype)


def ctx_attention(u, sink):
    wide = lambda col: pl.BlockSpec((pl.Element(SEQ), pl.Element(MIX_WIDTH)), lambda i: (i * SEQ, col))
    kv_w = B_KV_HEADS * HD
    narrow = lambda col: pl.BlockSpec((SEQ, kv_w), lambda i: (i, col // kv_w))
    out = jax.ShapeDtypeStruct((N_CTX_TOK, MIX_WIDTH), jnp.bfloat16)
    return pl.pallas_call(
        _ctx_attn_kernel,
        grid=(BATCH,),
        in_specs=[pl.BlockSpec(memory_space=pltpu.SMEM),
                  wide(U_QB), narrow(U_KB), narrow(U_VB), wide(U_QC), wide(U_KC), wide(U_VC)],
        out_specs=[pl.BlockSpec((SEQ, MIX_WIDTH), lambda i: (i, 0))] * 2,
        out_shape=[out, out],
        compiler_params=_params(("arbitrary",)),
        name="ctx_attention",
    )(sink, u, u, u, u, u, u)


def _rope_tables():
    half = HD // 2
    quarter = half // 2
    t = np.arange(DEC_SEQ)
    freqs = ROPE_BASE ** (-jnp.arange(quarter, dtype=jnp.float32) / quarter)
    pos = jnp.stack([jnp.asarray(t // GRID_W), jnp.asarray(t % GRID_W)], axis=1).astype(jnp.float32)
    ang = pos[:, :, None] * freqs[None, None, :]
    cos = jnp.cos(ang)
    sin = jnp.sin(ang)
    zero = jnp.zeros_like(sin)
    cos_t = jnp.concatenate([cos, cos], axis=2).reshape(DEC_SEQ, HD)
    sin_a = jnp.concatenate([-sin, zero], axis=2).reshape(DEC_SEQ, HD)
    sin_b = jnp.concatenate([zero, sin], axis=2).reshape(DEC_SEQ, HD)
    return cos_t, sin_a, sin_b


def _band_attn_kernel(sink_ref, qb_ref, kb_ref, vb_ref, ck_ref, cv_ref, cos_ref, sa_ref, sb_ref,
                      y_ref, q_scr, k_scr, v_scr):
    bf16 = jnp.bfloat16
    quarter = HD // 4
    cos, sa, sb = cos_ref[...], sa_ref[...], sb_ref[...]

    def rope(x):
        return x * cos + pltpu.roll(x, HD - quarter, 1) * sa + pltpu.roll(x, quarter, 1) * sb

    for h in range(B_HEADS):
        q_scr[:, h * HD:(h + 1) * HD] = rope(_head(qb_ref, h).astype(jnp.float32)).astype(bf16)
    pad = jnp.zeros((B_WINDOW, B_KV_HEADS * HD), bf16)
    for scr in (k_scr, v_scr):
        scr[0:B_WINDOW, :] = pad
        scr[B_WINDOW + DEC_SEQ:, :] = pad
    for kvh in range(B_KV_HEADS):
        k_scr[B_WINDOW:B_WINDOW + DEC_SEQ, kvh * HD:(kvh + 1) * HD] = rope(_head(kb_ref, kvh).astype(jnp.float32)).astype(bf16)
    v_scr[B_WINDOW:B_WINDOW + DEC_SEQ, :] = vb_ref[...].astype(bf16)
    ck = ck_ref[...].astype(bf16)
    cv = cv_ref[...].astype(bf16)

    span = B_BLOCK + 2 * B_WINDOW
    rows = GQA_GROUP * B_BLOCK
    i_idx = lax.broadcasted_iota(jnp.int32, (rows, span), 0) & (B_BLOCK - 1)
    c_idx = lax.broadcasted_iota(jnp.int32, (rows, span), 1)
    rel = c_idx - i_idx
    band_ok = (rel >= 0) & (rel <= 2 * B_WINDOW)
    grp = lax.broadcasted_iota(jnp.int32, (rows, 1), 0) // B_BLOCK

    def block(n, carry):
        r0 = pl.multiple_of(n * B_BLOCK, B_BLOCK)
        kpos = c_idx + (r0 - B_WINDOW)
        valid = band_ok & (kpos >= 0) & (kpos < DEC_SEQ)
        for kvh in range(B_KV_HEADS):
            heads = [kvh * GQA_GROUP + g for g in range(GQA_GROUP)]
            q = jnp.concatenate([q_scr[pl.ds(r0, B_BLOCK), h * HD:(h + 1) * HD] for h in heads], axis=0)
            ks = k_scr[pl.ds(r0, span), kvh * HD:(kvh + 1) * HD]
            vs = v_scr[pl.ds(r0, span), kvh * HD:(kvh + 1) * HD]
            s_loc = jnp.where(valid, _qk(q, ks), NEG_INF)
            s_ctx = _qk(q, ck[:, kvh * HD:(kvh + 1) * HD])
            sink = jnp.zeros((rows, 1), jnp.float32)
            for g, h in enumerate(heads):
                sink = jnp.where(grp == g, sink_ref[h], sink)
            o = _softmax_pv([(s_loc, vs), (s_ctx, cv[:, kvh * HD:(kvh + 1) * HD])], extra_logit=sink)
            for g, h in enumerate(heads):
                y_ref[pl.ds(r0, B_BLOCK), h * HD:(h + 1) * HD] = o[g * B_BLOCK:(g + 1) * B_BLOCK].astype(y_ref.dtype)
        return carry

    lax.fori_loop(0, DEC_SEQ // B_BLOCK, block, 0)


def band_attention(u, cache_k, cache_v, layer, sink, rope_tables):
    kv_w = B_KV_HEADS * HD
    ck = cache_k.reshape(DEC_BATCH, DEPTH, PAST_LEN, kv_w)
    cv = cache_v.reshape(DEC_BATCH, DEPTH, PAST_LEN, kv_w)
    cache_spec = pl.BlockSpec((None, None, PAST_LEN, kv_w), lambda b: (b, layer, 0, 0))
    table_spec = pl.BlockSpec((DEC_SEQ, HD), lambda b: (0, 0))
    return pl.pallas_call(
        _band_attn_kernel,
        grid=(DEC_BATCH,),
        in_specs=[pl.BlockSpec(memory_space=pltpu.SMEM),
                  pl.BlockSpec((DEC_SEQ, MIX_WIDTH), lambda b: (LAT_BLK0 + b, U_QB // MIX_WIDTH)),
                  pl.BlockSpec((DEC_SEQ, kv_w), lambda b: (LAT_BLK0 + b, U_KB // kv_w)),
                  pl.BlockSpec((DEC_SEQ, kv_w), lambda b: (LAT_BLK0 + b, U_VB // kv_w)),
                  cache_spec, cache_spec, table_spec, table_spec, table_spec],
        out_specs=pl.BlockSpec((DEC_SEQ, MIX_WIDTH), lambda b: (b, 0)),
        out_shape=jax.ShapeDtypeStruct((N_LAT_TOK, MIX_WIDTH), jnp.bfloat16),
        scratch_shapes=[pltpu.VMEM((DEC_SEQ, MIX_WIDTH), jnp.bfloat16),
                        pltpu.VMEM((DEC_SEQ + 2 * B_WINDOW, kv_w), jnp.bfloat16),
                        pltpu.VMEM((DEC_SEQ + 2 * B_WINDOW, kv_w), jnp.bfloat16)],
        compiler_params=_params(("arbitrary",)),
        name="band_attention",
    )(sink, u, u, u, ck, cv, *rope_tables)


def _na_bias_tables(na_rpb):
    col = np.arange(GRID_W)
    c0 = np.clip(col - NA_KW // 2, 0, GRID_W - NA_KW)
    col_ok = (col[None, :] >= c0[:, None]) & (col[None, :] < c0[:, None] + NA_KW)
    dc_i = np.clip(col[None, :] - col[:, None], -(NA_KW - 1), NA_KW - 1) + (NA_KW - 1)
    onehot = (np.arange(2 * NA_KW - 1)[:, None, None] == dc_i[None]).astype(np.float32)
    toeplitz = jnp.einsum('lhrc,cqk->lhrqk', na_rpb.astype(jnp.float32), onehot,
                          precision=lax.Precision.HIGHEST)
    toeplitz = jnp.where(col_ok, toeplitz, NEG_INF)
    masked = jnp.full((DEPTH, C_HEADS, GRID_W, GRID_W), NEG_INF, jnp.float32)
    blocks, _ = _na_query_blocks()
    tables = {}
    for rq0, lo, hi, off in blocks:
        if off in tables:
            continue
        rows = []
        for rq in range(rq0, rq0 + NA_Q_ROWS):
            first = _na_first_key_row(rq)
            pieces = [toeplitz[:, :, rk - rq + NA_KH - 1] if first <= rk < first + NA_KH else masked
                      for rk in range(lo // GRID_W, hi // GRID_W)]
            rows.append(jnp.concatenate(pieces, axis=-1))
        tables[off] = jnp.concatenate(rows, axis=-2)
    return jnp.concatenate([tables[off] for off in sorted(tables)], axis=-1)


def _na_first_key_row(rq):
    return min(max(rq - NA_KH // 2, 0), GRID_ROWS - NA_KH)


NA_Q_ROWS = 4


def _na_query_blocks():
    blocks, offsets, width = [], {}, 0
    for rq0 in range(0, GRID_ROWS, NA_Q_ROWS):
        lo = _na_first_key_row(rq0) * GRID_W // HD * HD
        hi = -(-(_na_first_key_row(rq0 + NA_Q_ROWS - 1) + NA_KH) * GRID_W // HD) * HD
        shape = (hi - lo, rq0 * GRID_W - lo) + tuple(_na_first_key_row(rq) * GRID_W - lo
                                                     for rq in range(rq0, rq0 + NA_Q_ROWS))
        if shape not in offsets:
            offsets[shape] = width
            width += hi - lo
        blocks.append((rq0, lo, hi, offsets[shape]))
    return blocks, width


def _na_attn_kernel(q_ref, k_ref, v_ref, ck_ref, cv_ref, tab_ref, y_ref):
    bf16 = jnp.bfloat16
    ck = ck_ref[...].astype(bf16)
    cv = cv_ref[...].astype(bf16)
    for rq0, lo, hi, off in _na_query_blocks()[0]:
        rows = slice(rq0 * GRID_W, (rq0 + NA_Q_ROWS) * GRID_W)
        q = q_ref[rows, :]
        s_loc = _qk(q, k_ref[lo:hi, :]) + tab_ref[:, off:off + hi - lo]
        o = _softmax_pv([(s_loc, v_ref[lo:hi, :]), (_qk(q, ck), cv)])
        y_ref[rows, :] = o.astype(y_ref.dtype)


def na_attention(u, cache_k, cache_v, layer, bias_tables):
    assert u.dtype == jnp.bfloat16
    ck = cache_k.reshape(DEC_BATCH, DEPTH, PAST_LEN, C_HEADS * HD)
    cv = cache_v.reshape(DEC_BATCH, DEPTH, PAST_LEN, C_HEADS * HD)
    cache_spec = pl.BlockSpec((None, None, PAST_LEN, HD), lambda h, b: (b, layer, 0, h))
    head = lambda col: pl.BlockSpec((DEC_SEQ, HD), lambda h, b: (LAT_BLK0 + b, col // HD + h))
    table_shape = bias_tables.shape[2:]
    return pl.pallas_call(
        _na_attn_kernel,
        grid=(C_HEADS, DEC_BATCH),
        in_specs=[head(U_QC), head(U_KC), head(U_VC), cache_spec, cache_spec,
                  pl.BlockSpec((None, None) + table_shape, lambda h, b: (layer, h, 0, 0))],
        out_specs=pl.BlockSpec((DEC_SEQ, HD), lambda h, b: (b, h)),
        out_shape=jax.ShapeDtypeStruct((N_LAT_TOK, MIX_WIDTH), jnp.bfloat16),
        compiler_params=_params(("arbitrary", "arbitrary")),
        name="na_attention",
    )(u, u, u, ck, cv, bias_tables)


def _pool_kernel(a_ref, w_ref, scale_ref, y_ref, pad_scr, *, seq):
    halo = SUBLANES
    zeros = jnp.zeros((halo, MIX_WIDTH), jnp.float32)
    pad_scr[0:halo, :] = zeros
    pad_scr[halo + seq:, :] = zeros
    pad_scr[halo:halo + seq, :] = a_ref[...].astype(jnp.float32)
    t = lax.broadcasted_iota(jnp.int32, (seq, 1), 0)
    for g, w in enumerate(POOL_WINDOWS):
        cols = slice(g * POOL_GROUP_DIM, (g + 1) * POOL_GROUP_DIM)
        lo, hi = -(w // 2), w - w // 2
        total = None
        for d in range(lo, hi):
            term = pad_scr[pl.ds(halo + d, seq), cols]
            total = term if total is None else total + term
        cnt = (jnp.minimum(t + hi, seq) - jnp.maximum(t + lo, 0)).astype(jnp.float32)
        diff = total / cnt - pad_scr[halo:halo + seq, cols]
        y = jnp.dot(diff.astype(jnp.bfloat16), w_ref[g], preferred_element_type=jnp.float32)
        y_ref[:, cols] = (y * scale_ref[:, cols]).astype(y_ref.dtype)


def pool_mix(u, w_pool, pool_scale, latent):
    seq, n_seq, blk0 = (DEC_SEQ, DEC_BATCH, LAT_BLK0) if latent else (SEQ, BATCH, 0)
    return pl.pallas_call(
        functools.partial(_pool_kernel, seq=seq),
        grid=(n_seq,),
        in_specs=[pl.BlockSpec((seq, MIX_WIDTH), lambda i: (blk0 + i, U_POOL // MIX_WIDTH)),
                  pl.BlockSpec((POOL_GROUPS, POOL_GROUP_DIM, POOL_GROUP_DIM), lambda i: (0, 0, 0)),
                  pl.BlockSpec((1, MIX_WIDTH), lambda i: (0, 0))],
        out_specs=pl.BlockSpec((seq, MIX_WIDTH), lambda i: (i, 0)),
        out_shape=jax.ShapeDtypeStruct((n_seq * seq, MIX_WIDTH), jnp.bfloat16),
        scratch_shapes=[pltpu.VMEM((seq + 2 * SUBLANES, MIX_WIDTH), jnp.float32)],
        compiler_params=_params(("arbitrary",)),
        name="pool_mix_latent" if latent else "pool_mix_ctx",
    )(u, w_pool, pool_scale)


DT_LANES = 128


def _split_bf16(x, parts):
    out = []
    rest = x
    for _ in range(parts):
        p = rest.astype(jnp.bfloat16)
        out.append(p)
        rest = rest - p.astype(jnp.float32)
    return out


def _select_rows(sel, x):
    return sum(jnp.dot(sel, p, preferred_element_type=jnp.float32) for p in _split_bf16(x, 3))


def _select_cols(x, sel):
    return sum(jnp.dot(p, sel, preferred_element_type=jnp.float32) for p in _split_bf16(x, 2))


def _ssd_kernel(xbc_ref, z_ref, dtr_ref, convw_ref, convb_ref, dtb_ref, alog_ref, dskip_ref, g_ref, *rest,
                seq, latent):
    if latent:
        h0_ref, y_ref, pad_scr, xs_scr, bc_scr, dt_scr, yacc_scr, h_scr = rest
    else:
        y_ref, hfin_ref, pad_scr, xs_scr, bc_scr, dt_scr, yacc_scr, h_scr = rest
    f32, bf16 = jnp.float32, jnp.bfloat16
    Q = SSM_CHUNK
    nc = seq // Q
    halo = SUBLANES
    gw = SSM_INNER // SSM_GROUPS
    pair_w = 2 * SSM_HEADDIM

    zeros = jnp.zeros((halo, SSM_XBC), f32)
    pad_scr[0:halo, :] = zeros
    pad_scr[halo + seq:, :] = zeros
    pad_scr[halo:halo + seq, :] = xbc_ref[...].astype(f32)
    for c in range(nc):
        acc = convb_ref[...]
        for k in range(SSM_CONV):
            start = halo + c * Q + k - SSM_CONV // 2
            acc = acc + convw_ref[k:k + 1, :] * pad_scr[start:start + Q, :]
        act = acc * jax.nn.sigmoid(acc)
        xs_scr[c * Q:(c + 1) * Q, :] = act[:, :SSM_INNER]
        bc_scr[c * Q:(c + 1) * Q, :] = act[:, SSM_INNER:]
        yacc_scr[c * Q:(c + 1) * Q, :] = act[:, :SSM_INNER] * dskip_ref[...]

    dt_lane = lax.broadcasted_iota(jnp.int32, (1, DT_LANES), 1) < 2 * SSM_HEADS
    pre = jnp.where(dt_lane, dtr_ref[...].astype(f32) + dtb_ref[...], 0.0)
    dt_scr[...] = jnp.maximum(pre, 0.0) + jnp.log1p(jnp.exp(-jnp.abs(pre)))
    a = -jnp.exp(alog_ref[...])

    if latent:
        h_scr[...] = h0_ref[...]
    else:
        h_scr[...] = jnp.zeros(h_scr.shape, f32)

    ri = lax.broadcasted_iota(jnp.int32, (Q, Q), 0)
    ci = lax.broadcasted_iota(jnp.int32, (Q, Q), 1)
    causal = (ci <= ri, ci >= ri)
    tri = tuple(m.astype(bf16) for m in causal)
    er = lax.broadcasted_iota(jnp.int32, (DT_LANES, SSM_INNER), 0)
    ec = lax.broadcasted_iota(jnp.int32, (DT_LANES, SSM_INNER), 1) // SSM_HEADDIM
    expand = tuple((er == ec + d * SSM_HEADS).astype(bf16) for d in range(2))
    lane = lax.broadcasted_iota(jnp.int32, (Q, pair_w), 1)
    first_head = lane < SSM_HEADDIM

    def step(s, carry):
        for d in range(2):
            c = s if d == 0 else nc - 1 - s
            r0 = pl.multiple_of(c * Q, Q)
            dtc = dt_scr[pl.ds(r0, Q), :]
            cum = _select_rows(tri[d], dtc * a)
            last = cum[Q - 1:Q, :] if d == 0 else cum[0:1, :]
            w_state = jnp.exp(last - cum) * dtc
            cum_t = cum.T
            dt_t = dtc.T
            w_state_x = _select_cols(w_state, expand[d])
            ecum_x = _select_cols(jnp.exp(cum), expand[d])
            decay_x = _select_cols(jnp.broadcast_to(jnp.exp(last), (SUBLANES, DT_LANES)), expand[d])[0:1, :]
            bcv = bc_scr[pl.ds(r0, Q), :]
            bmat = bcv[:, :SSM_GROUPS * SSM_STATE]
            cmat = bcv[:, SSM_GROUPS * SSM_STATE:]
            bmat_t = bmat.T
            xs_c = xs_scr[pl.ds(r0, Q), :]
            xw = (xs_c * w_state_x).astype(bf16)
            for g in range(SSM_GROUPS):
                st = slice(g * SSM_STATE, (g + 1) * SSM_STATE)
                ch = slice(g * gw, (g + 1) * gw)
                bg = bmat[:, st].astype(bf16)
                cg = cmat[:, st].astype(bf16)
                cb = lax.dot_general(cg, bg, _NT, preferred_element_type=f32)
                h_prev = h_scr[d, :, ch]
                y_inter = jnp.dot(cg, h_prev.astype(bf16), preferred_element_type=f32) * ecum_x[:, ch]
                new_state = jnp.dot(bmat_t[st, :].astype(bf16), xw[:, ch], preferred_element_type=f32)
                h_scr[d, :, ch] = h_prev * decay_x[:, ch] + new_state
                for k in range(gw // pair_w):
                    head = (g * gw + k * pair_w) // SSM_HEADDIM
                    w_pair = []
                    for ln in (d * SSM_HEADS + head, d * SSM_HEADS + head + 1):
                        col = jnp.broadcast_to(cum[:, ln:ln + 1], (Q, Q))
                        seg = jnp.exp(jnp.where(causal[d], col - cum_t[ln:ln + 1, :], NEG_INF))
                        w_pair.append(cb * seg * dt_t[ln:ln + 1, :])
                    lhs = jnp.concatenate(w_pair, axis=1).astype(bf16)
                    pc = slice(g * gw + k * pair_w, g * gw + (k + 1) * pair_w)
                    xp = xs_c[:, pc]
                    rhs = jnp.concatenate([jnp.where(first_head, xp, 0.0), jnp.where(first_head, 0.0, xp)],
                                          axis=0).astype(bf16)
                    y_pair = jnp.dot(lhs, rhs, preferred_element_type=f32) + y_inter[:, k * pair_w:(k + 1) * pair_w]
                    yacc_scr[pl.ds(r0, Q), pc] += y_pair
        return carry

    lax.fori_loop(0, nc, step, 0)

    for c in range(nc):
        rows = slice(c * Q, (c + 1) * Q)
        zc = z_ref[rows, :].astype(f32)
        y = yacc_scr[rows, :] * (zc * jax.nn.sigmoid(zc))
        ms = jnp.mean(jnp.square(y), axis=-1, keepdims=True)
        y_ref[rows, :] = (y * lax.rsqrt(ms + RMS_EPS) * g_ref[...]).astype(y_ref.dtype)
    if not latent:
        fill = jnp.zeros((HD - SSM_STATE, SSM_INNER), f32)
        for d in range(2):
            hfin_ref[d] = jnp.concatenate([h_scr[d], fill], axis=0).T[:, :SSM_STATE]


def _ssd_call(u, weights, seq, n_seq, blk0, h0=None):
    latent = h0 is not None
    state_shape = (2, SSM_STATE, SSM_INNER)
    vec = lambda n: pl.BlockSpec((1, n), lambda i: (0, 0))
    window = lambda col, width: pl.BlockSpec((pl.Element(seq), pl.Element(width)),
                                             lambda i: ((blk0 + i) * seq, col))
    in_specs = [window(U_XBC, SSM_XBC), window(U_Z, SSM_INNER),
                pl.BlockSpec((seq, DT_LANES), lambda i: (blk0 + i, U_DT // DT_LANES)),
                pl.BlockSpec((SUBLANES, SSM_XBC), lambda i: (0, 0)),
                vec(SSM_XBC), vec(DT_LANES), vec(DT_LANES), vec(SSM_INNER), vec(SSM_INNER)]
    args = [u, u, u, *weights]
    y_spec = pl.BlockSpec((seq, SSM_INNER), lambda i: (i, 0))
    y_shape = jax.ShapeDtypeStruct((n_seq * seq, SSM_INNER), jnp.bfloat16)
    if latent:
        in_specs.append(pl.BlockSpec((None,) + state_shape, lambda i: (i, 0, 0, 0)))
        args.append(h0)
        out_specs, out_shape = y_spec, y_shape
    else:
        final_shape = (2, SSM_INNER, SSM_STATE)
        out_specs = [y_spec, pl.BlockSpec((None,) + final_shape, lambda i: (i, 0, 0, 0))]
        out_shape = [y_shape, jax.ShapeDtypeStruct((n_seq,) + final_shape, jnp.float32)]
    f32 = jnp.float32
    return pl.pallas_call(
        functools.partial(_ssd_kernel, seq=seq, latent=latent),
        grid=(n_seq,),
        in_specs=in_specs,
        out_specs=out_specs,
        out_shape=out_shape,
        scratch_shapes=[pltpu.VMEM((seq + 2 * SUBLANES, SSM_XBC), f32), pltpu.VMEM((seq, SSM_INNER), f32),
                        pltpu.VMEM((seq, 2 * SSM_GROUPS * SSM_STATE), f32), pltpu.VMEM((seq, DT_LANES), f32),
                        pltpu.VMEM((seq, SSM_INNER), f32), pltpu.VMEM(state_shape, f32)],
        compiler_params=_params(("arbitrary",)),
        name="ssd_latent" if latent else "ssd_ctx",
    )(*args)


def ssd_mixer(u, conv_w, conv_b, a_log, dt_bias, d_skip, norm_g, state_l):
    f32 = jnp.float32
    lane_pad = lambda v: jnp.pad(v.reshape(1, -1).astype(f32), ((0, 0), (0, DT_LANES - v.size)))
    weights = (jnp.pad(conv_w.astype(f32), ((0, SUBLANES - SSM_CONV), (0, 0))), conv_b.reshape(1, -1),
               lane_pad(dt_bias), lane_pad(a_log), jnp.repeat(d_skip, SSM_HEADDIM).reshape(1, -1),
               norm_g.reshape(1, -1))
    h0 = state_l.transpose(0, 1, 4, 2, 3).reshape(DEC_BATCH, 2, SSM_STATE, SSM_INNER)
    yd_ctx, h_fin = _ssd_call(u, weights, SEQ, BATCH, 0)
    yd_lat = _ssd_call(u, weights, DEC_SEQ, DEC_BATCH, LAT_BLK0, h0=h0)
    return yd_ctx, yd_lat, h_fin.reshape(BATCH, 2, SSM_HEADS, SSM_HEADDIM, SSM_STATE)


def _pack_w_in(w_in_l):
    return w_in_l[:, :U_WIDTH].astype(jnp.bfloat16), w_in_l[:, U_GATES:].astype(jnp.bfloat16)


def kernel(x_prompt, x_sample, cache_b_k, cache_b_v, cache_c_k, cache_c_v, state_ssm, c, c_ctx, w_ada, b_ada, w_in, w_pool, pool_scale, attn_sink, na_rpb, conv_w, conv_b, a_log, dt_bias, d_skip, ssm_norm, w_branch, w_o, ln1_g, ln1_b, w_mlp1, w_mlp2, ln2_g, ln2_b):
    f32 = jnp.float32
    bf16 = jnp.bfloat16

    cond = jnp.concatenate([c_ctx[None, :], c, jnp.zeros((COND_PAD - N_COND, D_MODEL), f32)], axis=0)
    mod_all = ada_modulation(cond, w_ada, b_ada.reshape(DEPTH, 1, 6 * D_MODEL))
    blk_row = np.concatenate([np.zeros(N_CTX_TOK // MOD_ROWS, np.int32), 1 + np.arange(DEC_BATCH, dtype=np.int32)])
    mods = mod_all[:, blk_row].reshape(DEPTH, N_MOD_BLOCKS, 6, 1, D_MODEL).transpose(0, 2, 1, 3, 4)
    SHIFT1, SCALE1, GATE1, SHIFT2, SCALE2, GATE2 = range(6)

    x, h = modulate(x_prompt.reshape(N_CTX_TOK, D_MODEL), x_sample.reshape(N_LAT_TOK, D_MODEL),
                    mods[0], SCALE1, SHIFT1)
    rope_tables = _rope_tables()
    na_tables = _na_bias_tables(na_rpb)

    bk, bv, ckk, cvv, hs = [], [], [], [], []
    for l in range(DEPTH):
        w_u, w_gate = _pack_w_in(w_in[l])
        u = matmul(h, w_u, bf16, name="in_proj")

        pool_w = (w_pool[l].astype(bf16), pool_scale[l][None])
        ya_ctx = pool_mix(u, *pool_w, latent=False)
        ya_lat = pool_mix(u, *pool_w, latent=True)
        yb_ctx, yc_ctx = ctx_attention(u, attn_sink[l])
        yb_lat = band_attention(u, cache_b_k, cache_b_v, l, attn_sink[l], rope_tables)
        yc_lat = na_attention(u, cache_c_k, cache_c_v, l, na_tables)
        yd_ctx, yd_lat, h_ctx = ssd_mixer(u, conv_w[l], conv_b[l], a_log[l], dt_bias[l], d_skip[l], ssm_norm[l],
                                          state_ssm[:, l])

        def ctx_heads(col, heads):
            return u[:N_CTX_TOK, col:col + heads * HD].astype(f32).reshape(BATCH, SEQ, heads, HD)

        bk.append(ctx_heads(U_KB, B_KV_HEADS))
        bv.append(ctx_heads(U_VB, B_KV_HEADS))
        ckk.append(ctx_heads(U_KC, C_HEADS))
        cvv.append(ctx_heads(U_VC, C_HEADS))
        hs.append(h_ctx)

        merged = merge_branches(h, (ya_ctx, yb_ctx, yc_ctx, yd_ctx), (ya_lat, yb_lat, yc_lat, yd_lat),
                                w_gate, w_branch[l].astype(bf16))
        x, h2 = matmul_ln(merged, w_o[l].astype(bf16), x, mods[l], GATE1, ln1_g[l][None], ln1_b[l][None],
                          mods[l], SCALE2, SHIFT2, name="out_proj_ln1")
        ff = matmul(h2, w_mlp1[l].astype(bf16), bf16, relu_sq=True, name="mlp_up")
        nxt = min(l + 1, DEPTH - 1)
        x, h = matmul_ln(ff, w_mlp2[l].astype(bf16), x, mods[l], GATE2, ln2_g[l][None], ln2_b[l][None],
                         mods[nxt], SCALE1, SHIFT1, name="mlp_down_ln2")

    y_prompt = x[:N_CTX_TOK].reshape(BATCH, SEQ, D_MODEL)
    y_sample = x[N_CTX_TOK:].reshape(DEC_BATCH, DEC_SEQ, D_MODEL)
    return (y_prompt, y_sample, jnp.stack(bk, axis=1), jnp.stack(bv, axis=1), jnp.stack(ckk, axis=1),
            jnp.stack(cvv, axis=1), jnp.stack(hs, axis=1))
```

```python
import functools

import jax
import jax.numpy as jnp
import numpy as np
from jax import lax
from jax.experimental import pallas as pl
from jax.experimental.pallas import tpu as pltpu

D_MODEL = 2048
BATCH = 32
SEQ = 256
DEPTH = 4
DEC_BATCH = 8
DEC_SEQ = 1024
PAST_LEN = 256
GRID_W = 64
HD = 128
MIX_WIDTH = 1024
N_BRANCH = 4
POOL_GROUPS = 4
POOL_GROUP_DIM = MIX_WIDTH // POOL_GROUPS
POOL_WINDOWS = (2, 4, 8, 16)
B_HEADS = 8
B_KV_HEADS = 2
GQA_GROUP = B_HEADS // B_KV_HEADS
B_WINDOW = 128
B_BLOCK = 128
C_HEADS = 8
NA_KH = 8
NA_KW = 16
SSM_HEADS = 16
SSM_HEADDIM = 64
SSM_INNER = SSM_HEADS * SSM_HEADDIM
SSM_GROUPS = 2
SSM_STATE = 64
SSM_CONV = 5
SSM_CHUNK = 128
SSM_XBC = SSM_INNER + 2 * SSM_GROUPS * SSM_STATE
D_FF = 4 * D_MODEL
ROPE_BASE = 10000.0
LN_EPS = 1e-5
RMS_EPS = 1e-6
NEG_INF = -1e30
ALPHA = (2 * DEPTH) ** 0.25
ATTN_SCALE = HD ** -0.5

N_CTX_TOK = BATCH * SEQ
N_LAT_TOK = DEC_BATCH * DEC_SEQ
N_TOK = N_CTX_TOK + N_LAT_TOK
MOD_ROWS = DEC_SEQ
N_MOD_BLOCKS = N_TOK // MOD_ROWS
N_COND = 1 + DEC_BATCH
COND_PAD = 16
LAT_BLK0 = N_CTX_TOK // DEC_SEQ
GRID_ROWS = DEC_SEQ // GRID_W

U_POOL = 0
U_QB = U_POOL + MIX_WIDTH
U_KB = U_QB + B_HEADS * HD
U_VB = U_KB + B_KV_HEADS * HD
U_QC = U_VB + B_KV_HEADS * HD
U_KC = U_QC + C_HEADS * HD
U_VC = U_KC + C_HEADS * HD
U_Z = U_VC + C_HEADS * HD
U_XBC = U_Z + SSM_INNER
U_DT = U_XBC + SSM_XBC
U_GATES = U_DT + 2 * SSM_HEADS
U_WIDTH = 8192

SUBLANES = 8
VMEM_LIMIT = 56 * 1024 * 1024
_NT = (((1,), (1,)), ((), ()))


def _params(sem):
    return pltpu.CompilerParams(dimension_semantics=sem, vmem_limit_bytes=VMEM_LIMIT)


def _ada_kernel(cond_t_ref, w_ref, b_ref, o_ref):
    cond_t = cond_t_ref[...]
    s_t = (cond_t * jax.nn.sigmoid(cond_t)).astype(jnp.bfloat16)
    w = w_ref[...].astype(jnp.bfloat16)
    acc_t = lax.dot_general(w, s_t, (((0,), (0,)), ((), ())), preferred_element_type=jnp.float32)
    o_ref[...] = acc_t.T[:COND_PAD, :] + b_ref[...]


def ada_modulation(cond, w_ada, b_ada):
    bn = 2048
    n = w_ada.shape[-1]
    cond_t = jnp.pad(cond, ((0, HD - COND_PAD), (0, 0))).T
    return pl.pallas_call(
        _ada_kernel,
        grid=(DEPTH, n // bn),
        in_specs=[
            pl.BlockSpec((D_MODEL, HD), lambda l, j: (0, 0)),
            pl.BlockSpec((None, D_MODEL, bn), lambda l, j: (l, 0, j)),
            pl.BlockSpec((None, 1, bn), lambda l, j: (l, 0, j)),
        ],
        out_specs=pl.BlockSpec((None, COND_PAD, bn), lambda l, j: (l, 0, j)),
        out_shape=jax.ShapeDtypeStruct((DEPTH, COND_PAD, n), jnp.float32),
        compiler_params=_params(("arbitrary", "arbitrary")),
        name="ada_modulation",
    )(cond_t, w_ada, b_ada)


def _modulate_kernel(x_ctx_ref, x_lat_ref, scale_ref, shift_ref, x_ref, h_ref, *, n_ctx_blocks):
    def emit(src_ref):
        x = src_ref[...]
        x_ref[...] = x
        h_ref[...] = (x * (1.0 + scale_ref[...]) + shift_ref[...]).astype(h_ref.dtype)

    is_ctx = pl.program_id(0) < n_ctx_blocks
    pl.when(is_ctx)(lambda: emit(x_ctx_ref))
    pl.when(jnp.logical_not(is_ctx))(lambda: emit(x_lat_ref))


def _mod_spec(which, bm, lag=0):
    return pl.BlockSpec((None, None, 1, D_MODEL),
                        lambda i, *_: (which, (jnp.maximum(i - lag, 0) * bm) // MOD_ROWS, 0, 0))


def modulate(x_ctx, x_lat, mod, which_scale, which_shift):
    bm = 512
    n_ctx_blocks = N_CTX_TOK // bm
    row = pl.BlockSpec((bm, D_MODEL), lambda i: (i, 0))
    return pl.pallas_call(
        functools.partial(_modulate_kernel, n_ctx_blocks=n_ctx_blocks),
        grid=(N_TOK // bm,),
        in_specs=[
            pl.BlockSpec((bm, D_MODEL), lambda i: (jnp.minimum(i, n_ctx_blocks - 1), 0)),
            pl.BlockSpec((bm, D_MODEL), lambda i: (jnp.maximum(i - n_ctx_blocks, 0), 0)),
            _mod_spec(which_scale, bm),
            _mod_spec(which_shift, bm),
        ],
        out_specs=[row, row],
        out_shape=[jax.ShapeDtypeStruct((N_TOK, D_MODEL), jnp.float32),
                   jax.ShapeDtypeStruct((N_TOK, D_MODEL), jnp.bfloat16)],
        compiler_params=_params(("arbitrary",)),
        name="modulate",
    )(x_ctx, x_lat, mod, mod)


def _matmul_kernel(x_ref, w_ref, o_ref, *, relu_sq):
    acc = jnp.dot(x_ref[...], w_ref[...], preferred_element_type=jnp.float32)
    if relu_sq:
        acc = jnp.square(jnp.maximum(acc, 0.0))
    o_ref[...] = acc.astype(o_ref.dtype)


def matmul(x, w, out_dtype, relu_sq=False, bm=1024, bn=2048, name="matmul"):
    m, k = x.shape
    n = w.shape[1]
    return pl.pallas_call(
        functools.partial(_matmul_kernel, relu_sq=relu_sq),
        grid=(m // bm, n // bn),
        in_specs=[
            pl.BlockSpec((bm, k), lambda i, j: (i, 0)),
            pl.BlockSpec((k, bn), lambda i, j: (0, j)),
        ],
        out_specs=pl.BlockSpec((bm, bn), lambda i, j: (i, j)),
        out_shape=jax.ShapeDtypeStruct((m, n), out_dtype),
        compiler_params=_params(("arbitrary", "arbitrary")),
        name=name,
    )(x, w)


def _merge_kernel(h_ref, *refs, n_ctx_blocks):
    y_ctx, y_lat = refs[:N_BRANCH], refs[N_BRANCH:2 * N_BRANCH]
    wg_refs, (wb_ref, o_ref) = refs[2 * N_BRANCH:3 * N_BRANCH], refs[3 * N_BRANCH:]

    def merge(y_refs):
        h = h_ref[...]
        acc = None
        for n in range(N_BRANCH):
            gate = jax.nn.sigmoid(jnp.dot(h, wg_refs[n][...], preferred_element_type=jnp.float32))
            proj = jnp.dot(y_refs[n][...], wb_ref[n], preferred_element_type=jnp.float32)
            acc = gate * proj if acc is None else acc + gate * proj
        o_ref[...] = acc.astype(o_ref.dtype)

    is_ctx = pl.program_id(0) < n_ctx_blocks
    pl.when(is_ctx)(lambda: merge(y_ctx))
    pl.when(jnp.logical_not(is_ctx))(lambda: merge(y_lat))


def merge_branches(h, ys_ctx, ys_lat, w_gate, w_branch):
    bm, bn = 512, 512
    n_ctx_blocks = N_CTX_TOK // bm
    ctx_spec = pl.BlockSpec((bm, MIX_WIDTH), lambda i, j: (jnp.minimum(i, n_ctx_blocks - 1), 0))
    lat_spec = pl.BlockSpec((bm, MIX_WIDTH), lambda i, j: (jnp.maximum(i - n_ctx_blocks, 0), 0))
    gate_spec = lambda n: pl.BlockSpec((D_MODEL, bn), lambda i, j: (0, n * (D_MODEL // bn) + j))
    return pl.pallas_call(
        functools.partial(_merge_kernel, n_ctx_blocks=n_ctx_blocks),
        grid=(N_TOK // bm, D_MODEL // bn),
        in_specs=[
            pl.BlockSpec((bm, D_MODEL), lambda i, j: (i, 0)),
            *[ctx_spec] * N_BRANCH, *[lat_spec] * N_BRANCH,
            *[gate_spec(n) for n in range(N_BRANCH)],
            pl.BlockSpec((N_BRANCH, MIX_WIDTH, bn), lambda i, j: (0, 0, j)),
        ],
        out_specs=pl.BlockSpec((bm, bn), lambda i, j: (i, j)),
        out_shape=jax.ShapeDtypeStruct((N_TOK, D_MODEL), jnp.bfloat16),
        compiler_params=_params(("arbitrary", "arbitrary")),
        name="merge_branches",
    )(h, *ys_ctx, *ys_lat, *([w_gate] * N_BRANCH), w_branch)


def _matmul_ln_kernel(a_ref, w_ref, res_ref, gate_ref, g_ref, b_ref, scale_ref, shift_ref,
                      x_out_ref, h_out_ref, acc_even, acc_odd, *, nk):
    i = pl.program_id(0)
    k = pl.program_id(1)
    rows_per_step = acc_even.shape[0] // nk

    @pl.when((i == 0) & (k == 0))
    def _():
        acc_odd[...] = jnp.zeros(acc_odd.shape, jnp.float32)

    def step(acc_cur, acc_prev):
        part = jnp.dot(a_ref[...], w_ref[...], preferred_element_type=jnp.float32)
        if nk > 1:
            part = part + jnp.where(k > 0, acc_cur[...], 0.0)
        acc_cur[...] = part
        rows = pl.ds(pl.multiple_of(k * rows_per_step, rows_per_step), rows_per_step)
        r = ALPHA * res_ref[...] + gate_ref[...] * acc_prev[rows, :]
        mu = jnp.mean(r, axis=-1, keepdims=True)
        d = r - mu
        var = jnp.mean(jnp.square(d), axis=-1, keepdims=True)
        xn = d * lax.rsqrt(var + LN_EPS) * g_ref[...] + b_ref[...]
        x_out_ref[...] = xn
        h_out_ref[...] = (xn * (1.0 + scale_ref[...]) + shift_ref[...]).astype(h_out_ref.dtype)

    @pl.when(i % 2 == 0)
    def _():
        step(acc_even, acc_odd)

    @pl.when(i % 2 == 1)
    def _():
        step(acc_odd, acc_even)


def matmul_ln(a, w, res, mod, which_gate, ln_g, ln_b, mod_next, which_scale, which_shift, name):
    m, kdim = a.shape
    bm, bk = 1024, 1024
    nm, nk = m // bm, kdim // bk
    row = pl.BlockSpec((bm // nk, D_MODEL), lambda i, k: (jnp.where(i > 0, (i - 1) * nk + k, 0), 0))
    vec = pl.BlockSpec((1, D_MODEL), lambda i, k: (0, 0))
    acc = pltpu.VMEM((bm, D_MODEL), jnp.float32)
    return pl.pallas_call(
        functools.partial(_matmul_ln_kernel, nk=nk),
        grid=(nm + 1, nk),
        in_specs=[
            pl.BlockSpec((bm, bk), lambda i, k: (jnp.minimum(i, nm - 1), k)),
            pl.BlockSpec((bk, D_MODEL), lambda i, k: (k, 0)),
            row,
            _mod_spec(which_gate, bm, lag=1),
            vec, vec,
            _mod_spec(which_scale, bm, lag=1),
            _mod_spec(which_shift, bm, lag=1),
        ],
        out_specs=[row, row],
        out_shape=[jax.ShapeDtypeStruct((m, D_MODEL), jnp.float32),
                   jax.ShapeDtypeStruct((m, D_MODEL), jnp.bfloat16)],
        scratch_shapes=[acc, acc],
        compiler_params=_params(("arbitrary", "arbitrary")),
        name=name,
    )(a, w, res, mod, ln_g, ln_b, mod_next, mod_next)


def _softmax_pv(parts, extra_logit=None):
    m = None
    for s, _ in parts:
        mi = jnp.max(s, axis=-1, keepdims=True)
        m = mi if m is None else jnp.maximum(m, mi)
    if extra_logit is not None:
        m = jnp.maximum(m, extra_logit)
    es = [jnp.exp(s - m) for s, _ in parts]
    denom = None
    for e in es:
        li = jnp.sum(e, axis=-1, keepdims=True)
        denom = li if denom is None else denom + li
    if extra_logit is not None:
        denom = denom + jnp.exp(extra_logit - m)
    inv = 1.0 / denom
    out = None
    for e, (_, v) in zip(es, parts):
        o = jnp.dot((e * inv).astype(jnp.bfloat16), v, preferred_element_type=jnp.float32)
        out = o if out is None else out + o
    return out


def _qk(q, k):
    return lax.dot_general(q, k, _NT, preferred_element_type=jnp.float32) * ATTN_SCALE


def _head(ref, h):
    return ref[:, h * HD:(h + 1) * HD]


def _ctx_attn_kernel(sink_ref, qb_ref, kb_ref, vb_ref, qc_ref, kc_ref, vc_ref, yb_ref, yc_ref):
    bf16 = jnp.bfloat16
    for h in range(B_HEADS):
        kvh = h // GQA_GROUP
        q = _head(qb_ref, h).astype(bf16)
        k = _head(kb_ref, kvh).astype(bf16)
        v = _head(vb_ref, kvh).astype(bf16)
        o = _softmax_pv([(_qk(q, k), v)], extra_logit=sink_ref[h])
        yb_ref[:, h * HD:(h + 1) * HD] = o.astype(yb_ref.dtype)
    for h in range(C_HEADS):
        q = _head(qc_ref, h).astype(bf16)
        k = _head(kc_ref, h).astype(bf16)
        v = _head(vc_ref, h).astype(bf16)
        o = _softmax_pv([(_qk(q, k), v)])
        yc_ref[:, h * HD:(h + 1) * HD] = o.astype(yc_ref.dtype)


def ctx_attention(u, sink):
    wide = lambda col: pl.BlockSpec((pl.Element(SEQ), pl.Element(MIX_WIDTH)), lambda i: (i * SEQ, col))
    kv_w = B_KV_HEADS * HD
    narrow = lambda col: pl.BlockSpec((SEQ, kv_w), lambda i: (i, col // kv_w))
    out = jax.ShapeDtypeStruct((N_CTX_TOK, MIX_WIDTH), jnp.bfloat16)
    return pl.pallas_call(
        _ctx_attn_kernel,
        grid=(BATCH,),
        in_specs=[pl.BlockSpec(memory_space=pltpu.SMEM),
                  wide(U_QB), narrow(U_KB), narrow(U_VB), wide(U_QC), wide(U_KC), wide(U_VC)],
        out_specs=[pl.BlockSpec((SEQ, MIX_WIDTH), lambda i: (i, 0))] * 2,
        out_shape=[out, out],
        compiler_params=_params(("arbitrary",)),
        name="ctx_attention",
    )(sink, u, u, u, u, u, u)


def _rope_tables():
    half = HD // 2
    quarter = half // 2
    t = np.arange(DEC_SEQ)
    freqs = ROPE_BASE ** (-jnp.arange(quarter, dtype=jnp.float32) / quarter)
    pos = jnp.stack([jnp.asarray(t // GRID_W), jnp.asarray(t % GRID_W)], axis=1).astype(jnp.float32)
    ang = pos[:, :, None] * freqs[None, None, :]
    cos = jnp.cos(ang)
    sin = jnp.sin(ang)
    zero = jnp.zeros_like(sin)
    cos_t = jnp.concatenate([cos, cos], axis=2).reshape(DEC_SEQ, HD)
    sin_a = jnp.concatenate([-sin, zero], axis=2).reshape(DEC_SEQ, HD)
    sin_b = jnp.concatenate([zero, sin], axis=2).reshape(DEC_SEQ, HD)
    return cos_t, sin_a, sin_b


def _band_attn_kernel(sink_ref, qb_ref, kb_ref, vb_ref, ck_ref, cv_ref, cos_ref, sa_ref, sb_ref,
                      y_ref, q_scr, k_scr, v_scr):
    bf16 = jnp.bfloat16
    quarter = HD // 4
    cos, sa, sb = cos_ref[...], sa_ref[...], sb_ref[...]

    def rope(x):
        return x * cos + pltpu.roll(x, HD - quarter, 1) * sa + pltpu.roll(x, quarter, 1) * sb

    for h in range(B_HEADS):
        q_scr[:, h * HD:(h + 1) * HD] = rope(_head(qb_ref, h).astype(jnp.float32)).astype(bf16)
    pad = jnp.zeros((B_WINDOW, B_KV_HEADS * HD), bf16)
    for scr in (k_scr, v_scr):
        scr[0:B_WINDOW, :] = pad
        scr[B_WINDOW + DEC_SEQ:, :] = pad
    for kvh in range(B_KV_HEADS):
        k_scr[B_WINDOW:B_WINDOW + DEC_SEQ, kvh * HD:(kvh + 1) * HD] = rope(_head(kb_ref, kvh).astype(jnp.float32)).astype(bf16)
    v_scr[B_WINDOW:B_WINDOW + DEC_SEQ, :] = vb_ref[...].astype(bf16)
    ck = ck_ref[...].astype(bf16)
    cv = cv_ref[...].astype(bf16)

    span = B_BLOCK + 2 * B_WINDOW
    rows = GQA_GROUP * B_BLOCK
    i_idx = lax.broadcasted_iota(jnp.int32, (rows, span), 0) & (B_BLOCK - 1)
    c_idx = lax.broadcasted_iota(jnp.int32, (rows, span), 1)
    rel = c_idx - i_idx
    band_ok = (rel >= 0) & (rel <= 2 * B_WINDOW)
    grp = lax.broadcasted_iota(jnp.int32, (rows, 1), 0) // B_BLOCK

    def block(n, carry):
        r0 = pl.multiple_of(n * B_BLOCK, B_BLOCK)
        kpos = c_idx + (r0 - B_WINDOW)
        valid = band_ok & (kpos >= 0) & (kpos < DEC_SEQ)
        for kvh in range(B_KV_HEADS):
            heads = [kvh * GQA_GROUP + g for g in range(GQA_GROUP)]
            q = jnp.concatenate([q_scr[pl.ds(r0, B_BLOCK), h * HD:(h + 1) * HD] for h in heads], axis=0)
            ks = k_scr[pl.ds(r0, span), kvh * HD:(kvh + 1) * HD]
            vs = v_scr[pl.ds(r0, span), kvh * HD:(kvh + 1) * HD]
            s_loc = jnp.where(valid, _qk(q, ks), NEG_INF)
            s_ctx = _qk(q, ck[:, kvh * HD:(kvh + 1) * HD])
            sink = jnp.zeros((rows, 1), jnp.float32)
            for g, h in enumerate(heads):
                sink = jnp.where(grp == g, sink_ref[h], sink)
            o = _softmax_pv([(s_loc, vs), (s_ctx, cv[:, kvh * HD:(kvh + 1) * HD])], extra_logit=sink)
            for g, h in enumerate(heads):
                y_ref[pl.ds(r0, B_BLOCK), h * HD:(h + 1) * HD] = o[g * B_BLOCK:(g + 1) * B_BLOCK].astype(y_ref.dtype)
        return carry

    lax.fori_loop(0, DEC_SEQ // B_BLOCK, block, 0)


def band_attention(u, cache_k, cache_v, layer, sink, rope_tables):
    kv_w = B_KV_HEADS * HD
    ck = cache_k.reshape(DEC_BATCH, DEPTH, PAST_LEN, kv_w)
    cv = cache_v.reshape(DEC_BATCH, DEPTH, PAST_LEN, kv_w)
    cache_spec = pl.BlockSpec((None, None, PAST_LEN, kv_w), lambda b: (b, layer, 0, 0))
    table_spec = pl.BlockSpec((DEC_SEQ, HD), lambda b: (0, 0))
    return pl.pallas_call(
        _band_attn_kernel,
        grid=(DEC_BATCH,),
        in_specs=[pl.BlockSpec(memory_space=pltpu.SMEM),
                  pl.BlockSpec((DEC_SEQ, MIX_WIDTH), lambda b: (LAT_BLK0 + b, U_QB // MIX_WIDTH)),
                  pl.BlockSpec((DEC_SEQ, kv_w), lambda b: (LAT_BLK0 + b, U_KB // kv_w)),
                  pl.BlockSpec((DEC_SEQ, kv_w), lambda b: (LAT_BLK0 + b, U_VB // kv_w)),
                  cache_spec, cache_spec, table_spec, table_spec, table_spec],
        out_specs=pl.BlockSpec((DEC_SEQ, MIX_WIDTH), lambda b: (b, 0)),
        out_shape=jax.ShapeDtypeStruct((N_LAT_TOK, MIX_WIDTH), jnp.bfloat16),
        scratch_shapes=[pltpu.VMEM((DEC_SEQ, MIX_WIDTH), jnp.bfloat16),
                        pltpu.VMEM((DEC_SEQ + 2 * B_WINDOW, kv_w), jnp.bfloat16),
                        pltpu.VMEM((DEC_SEQ + 2 * B_WINDOW, kv_w), jnp.bfloat16)],
        compiler_params=_params(("arbitrary",)),
        name="band_attention",
    )(sink, u, u, u, ck, cv, *rope_tables)


def _na_bias_tables(na_rpb):
    col = np.arange(GRID_W)
    c0 = np.clip(col - NA_KW // 2, 0, GRID_W - NA_KW)
    col_ok = (col[None, :] >= c0[:, None]) & (col[None, :] < c0[:, None] + NA_KW)
    dc_i = np.clip(col[None, :] - col[:, None], -(NA_KW - 1), NA_KW - 1) + (NA_KW - 1)
    onehot = (np.arange(2 * NA_KW - 1)[:, None, None] == dc_i[None]).astype(np.float32)
    toeplitz = jnp.einsum('lhrc,cqk->lhrqk', na_rpb.astype(jnp.float32), onehot,
                          precision=lax.Precision.HIGHEST)
    toeplitz = jnp.where(col_ok, toeplitz, NEG_INF)
    masked = jnp.full((DEPTH, C_HEADS, GRID_W, GRID_W), NEG_INF, jnp.float32)
    blocks, _ = _na_query_blocks()
    tables = {}
    for rq0, lo, hi, off in blocks:
        if off in tables:
            continue
        rows = []
        for rq in range(rq0, rq0 + NA_Q_ROWS):
            first = _na_first_key_row(rq)
            pieces = [toeplitz[:, :, rk - rq + NA_KH - 1] if first <= rk < first + NA_KH else masked
                      for rk in range(lo // GRID_W, hi // GRID_W)]
            rows.append(jnp.concatenate(pieces, axis=-1))
        tables[off] = jnp.concatenate(rows, axis=-2)
    return jnp.concatenate([tables[off] for off in sorted(tables)], axis=-1)


def _na_first_key_row(rq):
    return min(max(rq - NA_KH // 2, 0), GRID_ROWS - NA_KH)


NA_Q_ROWS = 4


def _na_query_blocks():
    blocks, offsets, width = [], {}, 0
    for rq0 in range(0, GRID_ROWS, NA_Q_ROWS):
        lo = _na_first_key_row(rq0) * GRID_W // HD * HD
        hi = -(-(_na_first_key_row(rq0 + NA_Q_ROWS - 1) + NA_KH) * GRID_W // HD) * HD
        shape = (hi - lo, rq0 * GRID_W - lo) + tuple(_na_first_key_row(rq) * GRID_W - lo
                                                     for rq in range(rq0, rq0 + NA_Q_ROWS))
        if shape not in offsets:
            offsets[shape] = width
            width += hi - lo
        blocks.append((rq0, lo, hi, offsets[shape]))
    return blocks, width


def _na_attn_kernel(q_ref, k_ref, v_ref, ck_ref, cv_ref, tab_ref, y_ref):
    bf16 = jnp.bfloat16
    ck = ck_ref[...].astype(bf16)
    cv = cv_ref[...].astype(bf16)
    for rq0, lo, hi, off in _na_query_blocks()[0]:
        rows = slice(rq0 * GRID_W, (rq0 + NA_Q_ROWS) * GRID_W)
        q = q_ref[rows, :]
        s_loc = _qk(q, k_ref[lo:hi, :]) + tab_ref[:, off:off + hi - lo]
        o = _softmax_pv([(s_loc, v_ref[lo:hi, :]), (_qk(q, ck), cv)])
        y_ref[rows, :] = o.astype(y_ref.dtype)


def na_attention(u, cache_k, cache_v, layer, bias_tables):
    assert u.dtype == jnp.bfloat16
    ck = cache_k.reshape(DEC_BATCH, DEPTH, PAST_LEN, C_HEADS * HD)
    cv = cache_v.reshape(DEC_BATCH, DEPTH, PAST_LEN, C_HEADS * HD)
    cache_spec = pl.BlockSpec((None, None, PAST_LEN, HD), lambda h, b: (b, layer, 0, h))
    head = lambda col: pl.BlockSpec((DEC_SEQ, HD), lambda h, b: (LAT_BLK0 + b, col // HD + h))
    table_shape = bias_tables.shape[2:]
    return pl.pallas_call(
        _na_attn_kernel,
        grid=(C_HEADS, DEC_BATCH),
        in_specs=[head(U_QC), head(U_KC), head(U_VC), cache_spec, cache_spec,
                  pl.BlockSpec((None, None) + table_shape, lambda h, b: (layer, h, 0, 0))],
        out_specs=pl.BlockSpec((DEC_SEQ, HD), lambda h, b: (b, h)),
        out_shape=jax.ShapeDtypeStruct((N_LAT_TOK, MIX_WIDTH), jnp.bfloat16),
        compiler_params=_params(("arbitrary", "arbitrary")),
        name="na_attention",
    )(u, u, u, ck, cv, bias_tables)


POOL_BLOCK = 128
POOL_HALO = 16


def _pool_kernel(a_ref, w_ref, scale_ref, y_ref, pad_scr, *, seq):
    bf16, f32 = jnp.bfloat16, jnp.float32
    assert a_ref.dtype == bf16
    halo, blk = POOL_HALO, POOL_BLOCK
    span = blk + 2 * halo
    zeros = jnp.zeros((halo, MIX_WIDTH), bf16)
    pad_scr[0:halo, :] = zeros
    pad_scr[halo + seq:, :] = zeros
    pad_scr[halo:halo + seq, :] = a_ref[...]
    rel = (lax.broadcasted_iota(jnp.int32, (blk, span), 1) - halo) - lax.broadcasted_iota(jnp.int32, (blk, span), 0)
    bands = [((rel >= -(w // 2)) & (rel < w - w // 2)).astype(bf16) for w in POOL_WINDOWS]
    for r0 in range(0, seq, blk):
        x_span = pad_scr[r0:r0 + span, :]
        a_blk = a_ref[r0:r0 + blk, :].astype(f32)
        t = r0 + lax.broadcasted_iota(jnp.int32, (blk, 1), 0)
        for g, w in enumerate(POOL_WINDOWS):
            cols = slice(g * POOL_GROUP_DIM, (g + 1) * POOL_GROUP_DIM)
            lo, hi = -(w // 2), w - w // 2
            total = jnp.dot(bands[g], x_span[:, cols], preferred_element_type=f32)
            cnt = (jnp.minimum(t + hi, seq) - jnp.maximum(t + lo, 0)).astype(f32)
            diff = total / cnt - a_blk[:, cols]
            y = jnp.dot(diff.astype(bf16), w_ref[g], preferred_element_type=f32)
            y_ref[r0:r0 + blk, cols] = (y * scale_ref[:, cols]).astype(y_ref.dtype)


def pool_mix(u, w_pool, pool_scale, latent):
    seq, n_seq, blk0 = (DEC_SEQ, DEC_BATCH, LAT_BLK0) if latent else (SEQ, BATCH, 0)
    return pl.pallas_call(
        functools.partial(_pool_kernel, seq=seq),
        grid=(n_seq,),
        in_specs=[pl.BlockSpec((seq, MIX_WIDTH), lambda i: (blk0 + i, U_POOL // MIX_WIDTH)),
                  pl.BlockSpec((POOL_GROUPS, POOL_GROUP_DIM, POOL_GROUP_DIM), lambda i: (0, 0, 0)),
                  pl.BlockSpec((1, MIX_WIDTH), lambda i: (0, 0))],
        out_specs=pl.BlockSpec((seq, MIX_WIDTH), lambda i: (i, 0)),
        out_shape=jax.ShapeDtypeStruct((n_seq * seq, MIX_WIDTH), jnp.bfloat16),
        scratch_shapes=[pltpu.VMEM((seq + 2 * POOL_HALO, MIX_WIDTH), jnp.bfloat16)],
        compiler_params=_params(("arbitrary",)),
        name="pool_mix_latent" if latent else "pool_mix_ctx",
    )(u, w_pool, pool_scale)


DT_LANES = 128
CONV_HALO = 16


def _split_bf16(x, parts):
    out = []
    rest = x
    for _ in range(parts):
        p = rest.astype(jnp.bfloat16)
        out.append(p)
        rest = rest - p.astype(jnp.float32)
    return out


def _select_rows(sel, x):
    return sum(jnp.dot(sel, p, preferred_element_type=jnp.float32) for p in _split_bf16(x, 3))


def _select_cols(x, sel):
    return sum(jnp.dot(p, sel, preferred_element_type=jnp.float32) for p in _split_bf16(x, 2))


def _ssd_kernel(xbc_ref, z_ref, dtr_ref, convw_ref, convb_ref, dtb_ref, alog_ref, dskip_ref, g_ref, *rest,
                seq, latent):
    if latent:
        h0_ref, y_ref, pad_scr, xs_scr, bc_scr, dt_scr, yacc_scr, h_scr = rest
    else:
        y_ref, hfin_ref, pad_scr, xs_scr, bc_scr, dt_scr, yacc_scr, h_scr = rest
    f32, bf16 = jnp.float32, jnp.bfloat16
    Q = SSM_CHUNK
    nc = seq // Q
    halo = CONV_HALO
    gw = SSM_INNER // SSM_GROUPS
    pair_w = 2 * SSM_HEADDIM

    assert xbc_ref.dtype == bf16
    zeros = jnp.zeros((halo, SSM_XBC), bf16)
    pad_scr[0:halo, :] = zeros
    pad_scr[halo + seq:, :] = zeros
    pad_scr[halo:halo + seq, :] = xbc_ref[...]
    span = Q + 2 * halo
    rel = (lax.broadcasted_iota(jnp.int32, (Q, span), 1) - halo) - lax.broadcasted_iota(jnp.int32, (Q, span), 0)
    for c in range(nc):
        x_span = pad_scr[c * Q:c * Q + span, :]
        acc = convb_ref[...]
        for k in range(SSM_CONV):
            off = k - SSM_CONV // 2
            if off == 0:
                tap = x_span[halo:halo + Q, :].astype(f32)
            else:
                tap = jnp.dot((rel == off).astype(bf16), x_span, preferred_element_type=f32)
            acc = acc + convw_ref[k:k + 1, :] * tap
        act = acc * jax.nn.sigmoid(acc)
        xs_scr[c * Q:(c + 1) * Q, :] = act[:, :SSM_INNER]
        bc_scr[c * Q:(c + 1) * Q, :] = act[:, SSM_INNER:]
        yacc_scr[c * Q:(c + 1) * Q, :] = act[:, :SSM_INNER] * dskip_ref[...]

    dt_lane = lax.broadcasted_iota(jnp.int32, (1, DT_LANES), 1) < 2 * SSM_HEADS
    pre = jnp.where(dt_lane, dtr_ref[...].astype(f32) + dtb_ref[...], 0.0)
    dt_scr[...] = jnp.maximum(pre, 0.0) + jnp.log1p(jnp.exp(-jnp.abs(pre)))
    a = -jnp.exp(alog_ref[...])

    if latent:
        h_scr[...] = h0_ref[...]
    else:
        h_scr[...] = jnp.zeros(h_scr.shape, f32)

    ri = lax.broadcasted_iota(jnp.int32, (Q, Q), 0)
    ci = lax.broadcasted_iota(jnp.int32, (Q, Q), 1)
    causal = (ci <= ri, ci >= ri)
    tri = tuple(m.astype(bf16) for m in causal)
    er = lax.broadcasted_iota(jnp.int32, (DT_LANES, SSM_INNER), 0)
    ec = lax.broadcasted_iota(jnp.int32, (DT_LANES, SSM_INNER), 1) // SSM_HEADDIM
    expand = tuple((er == ec + d * SSM_HEADS).astype(bf16) for d in range(2))
    lane = lax.broadcasted_iota(jnp.int32, (Q, pair_w), 1)
    first_head = lane < SSM_HEADDIM

    def step(s, carry):
        for d in range(2):
            c = s if d == 0 else nc - 1 - s
            r0 = pl.multiple_of(c * Q, Q)
            dtc = dt_scr[pl.ds(r0, Q), :]
            cum = _select_rows(tri[d], dtc * a)
            last = cum[Q - 1:Q, :] if d == 0 else cum[0:1, :]
            w_state = jnp.exp(last - cum) * dtc
            cum_t = cum.T
            dt_t = dtc.T
            w_state_x = _select_cols(w_state, expand[d])
            ecum_x = _select_cols(jnp.exp(cum), expand[d])
            decay_x = _select_cols(jnp.broadcast_to(jnp.exp(last), (SUBLANES, DT_LANES)), expand[d])[0:1, :]
            bcv = bc_scr[pl.ds(r0, Q), :]
            bmat = bcv[:, :SSM_GROUPS * SSM_STATE]
            cmat = bcv[:, SSM_GROUPS * SSM_STATE:]
            bmat_t = bmat.T
            xs_c = xs_scr[pl.ds(r0, Q), :]
            xw = (xs_c * w_state_x).astype(bf16)
            for g in range(SSM_GROUPS):
                st = slice(g * SSM_STATE, (g + 1) * SSM_STATE)
                ch = slice(g * gw, (g + 1) * gw)
                bg = bmat[:, st].astype(bf16)
                cg = cmat[:, st].astype(bf16)
                cb = lax.dot_general(cg, bg, _NT, preferred_element_type=f32)
                h_prev = h_scr[d, :, ch]
                y_inter = jnp.dot(cg, h_prev.astype(bf16), preferred_element_type=f32) * ecum_x[:, ch]
                new_state = jnp.dot(bmat_t[st, :].astype(bf16), xw[:, ch], preferred_element_type=f32)
                h_scr[d, :, ch] = h_prev * decay_x[:, ch] + new_state
                for k in range(gw // pair_w):
                    head = (g * gw + k * pair_w) // SSM_HEADDIM
                    w_pair = []
                    for ln in (d * SSM_HEADS + head, d * SSM_HEADS + head + 1):
                        col = jnp.broadcast_to(cum[:, ln:ln + 1], (Q, Q))
                        seg = jnp.exp(jnp.where(causal[d], col - cum_t[ln:ln + 1, :], NEG_INF))
                        w_pair.append(cb * seg * dt_t[ln:ln + 1, :])
                    lhs = jnp.concatenate(w_pair, axis=1).astype(bf16)
                    pc = slice(g * gw + k * pair_w, g * gw + (k + 1) * pair_w)
                    xp = xs_c[:, pc]
                    rhs = jnp.concatenate([jnp.where(first_head, xp, 0.0), jnp.where(first_head, 0.0, xp)],
                                          axis=0).astype(bf16)
                    y_pair = jnp.dot(lhs, rhs, preferred_element_type=f32) + y_inter[:, k * pair_w:(k + 1) * pair_w]
                    yacc_scr[pl.ds(r0, Q), pc] += y_pair
        return carry

    lax.fori_loop(0, nc, step, 0)

    for c in range(nc):
        rows = slice(c * Q, (c + 1) * Q)
        zc = z_ref[rows, :].astype(f32)
        y = yacc_scr[rows, :] * (zc * jax.nn.sigmoid(zc))
        ms = jnp.mean(jnp.square(y), axis=-1, keepdims=True)
        y_ref[rows, :] = (y * lax.rsqrt(ms + RMS_EPS) * g_ref[...]).astype(y_ref.dtype)
    if not latent:
        fill = jnp.zeros((HD - SSM_STATE, SSM_INNER), f32)
        for d in range(2):
            hfin_ref[d] = jnp.concatenate([h_scr[d], fill], axis=0).T[:, :SSM_STATE]


def _ssd_call(u, weights, seq, n_seq, blk0, h0=None):
    latent = h0 is not None
    state_shape = (2, SSM_STATE, SSM_INNER)
    vec = lambda n: pl.BlockSpec((1, n), lambda i: (0, 0))
    window = lambda col, width: pl.BlockSpec((pl.Element(seq), pl.Element(width)),
                                             lambda i: ((blk0 + i) * seq, col))
    in_specs = [window(U_XBC, SSM_XBC), window(U_Z, SSM_INNER),
                pl.BlockSpec((seq, DT_LANES), lambda i: (blk0 + i, U_DT // DT_LANES)),
                pl.BlockSpec((SUBLANES, SSM_XBC), lambda i: (0, 0)),
                vec(SSM_XBC), vec(DT_LANES), vec(DT_LANES), vec(SSM_INNER), vec(SSM_INNER)]
    args = [u, u, u, *weights]
    y_spec = pl.BlockSpec((seq, SSM_INNER), lambda i: (i, 0))
    y_shape = jax.ShapeDtypeStruct((n_seq * seq, SSM_INNER), jnp.bfloat16)
    if latent:
        in_specs.append(pl.BlockSpec((None,) + state_shape, lambda i: (i, 0, 0, 0)))
        args.append(h0)
        out_specs, out_shape = y_spec, y_shape
    else:
        final_shape = (2, SSM_INNER, SSM_STATE)
        out_specs = [y_spec, pl.BlockSpec((None,) + final_shape, lambda i: (i, 0, 0, 0))]
        out_shape = [y_shape, jax.ShapeDtypeStruct((n_seq,) + final_shape, jnp.float32)]
    f32 = jnp.float32
    return pl.pallas_call(
        functools.partial(_ssd_kernel, seq=seq, latent=latent),
        grid=(n_seq,),
        in_specs=in_specs,
        out_specs=out_specs,
        out_shape=out_shape,
        scratch_shapes=[pltpu.VMEM((seq + 2 * CONV_HALO, SSM_XBC), jnp.bfloat16), pltpu.VMEM((seq, SSM_INNER), f32),
                        pltpu.VMEM((seq, 2 * SSM_GROUPS * SSM_STATE), f32), pltpu.VMEM((seq, DT_LANES), f32),
                        pltpu.VMEM((seq, SSM_INNER), f32), pltpu.VMEM(state_shape, f32)],
        compiler_params=_params(("arbitrary",)),
        name="ssd_latent" if latent else "ssd_ctx",
    )(*args)


def ssd_mixer(u, conv_w, conv_b, a_log, dt_bias, d_skip, norm_g, state_l):
    f32 = jnp.float32
    lane_pad = lambda v: jnp.pad(v.reshape(1, -1).astype(f32), ((0, 0), (0, DT_LANES - v.size)))
    weights = (jnp.pad(conv_w.astype(f32), ((0, SUBLANES - SSM_CONV), (0, 0))), conv_b.reshape(1, -1),
               lane_pad(dt_bias), lane_pad(a_log), jnp.repeat(d_skip, SSM_HEADDIM).reshape(1, -1),
               norm_g.reshape(1, -1))
    h0 = state_l.transpose(0, 1, 4, 2, 3).reshape(DEC_BATCH, 2, SSM_STATE, SSM_INNER)
    yd_ctx, h_fin = _ssd_call(u, weights, SEQ, BATCH, 0)
    yd_lat = _ssd_call(u, weights, DEC_SEQ, DEC_BATCH, LAT_BLK0, h0=h0)
    return yd_ctx, yd_lat, h_fin.reshape(BATCH, 2, SSM_HEADS, SSM_HEADDIM, SSM_STATE)


def _pack_w_in(w_in_l):
    return w_in_l[:, :U_WIDTH].astype(jnp.bfloat16), w_in_l[:, U_GATES:].astype(jnp.bfloat16)


def kernel(x_prompt, x_sample, cache_b_k, cache_b_v, cache_c_k, cache_c_v, state_ssm, c, c_ctx, w_ada, b_ada, w_in, w_pool, pool_scale, attn_sink, na_rpb, conv_w, conv_b, a_log, dt_bias, d_skip, ssm_norm, w_branch, w_o, ln1_g, ln1_b, w_mlp1, w_mlp2, ln2_g, ln2_b):
    f32 = jnp.float32
    bf16 = jnp.bfloat16

    cond = jnp.concatenate([c_ctx[None, :], c, jnp.zeros((COND_PAD - N_COND, D_MODEL), f32)], axis=0)
    mod_all = ada_modulation(cond, w_ada, b_ada.reshape(DEPTH, 1, 6 * D_MODEL))
    blk_row = np.concatenate([np.zeros(N_CTX_TOK // MOD_ROWS, np.int32), 1 + np.arange(DEC_BATCH, dtype=np.int32)])
    mods = mod_all[:, blk_row].reshape(DEPTH, N_MOD_BLOCKS, 6, 1, D_MODEL).transpose(0, 2, 1, 3, 4)
    SHIFT1, SCALE1, GATE1, SHIFT2, SCALE2, GATE2 = range(6)

    x, h = modulate(x_prompt.reshape(N_CTX_TOK, D_MODEL), x_sample.reshape(N_LAT_TOK, D_MODEL),
                    mods[0], SCALE1, SHIFT1)
    rope_tables = _rope_tables()
    na_tables = _na_bias_tables(na_rpb)

    bk, bv, ckk, cvv, hs = [], [], [], [], []
    for l in range(DEPTH):
        w_u, w_gate = _pack_w_in(w_in[l])
        u = matmul(h, w_u, bf16, name="in_proj")

        pool_w = (w_pool[l].astype(bf16), pool_scale[l][None])
        ya_ctx = pool_mix(u, *pool_w, latent=False)
        ya_lat = pool_mix(u, *pool_w, latent=True)
        yb_ctx, yc_ctx = ctx_attention(u, attn_sink[l])
        yb_lat = band_attention(u, cache_b_k, cache_b_v, l, attn_sink[l], rope_tables)
        yc_lat = na_attention(u, cache_c_k, cache_c_v, l, na_tables)
        yd_ctx, yd_lat, h_ctx = ssd_mixer(u, conv_w[l], conv_b[l], a_log[l], dt_bias[l], d_skip[l], ssm_norm[l],
                                          state_ssm[:, l])

        def ctx_heads(col, heads):
            return u[:N_CTX_TOK, col:col + heads * HD].astype(f32).reshape(BATCH, SEQ, heads, HD)

        bk.append(ctx_heads(U_KB, B_KV_HEADS))
        bv.append(ctx_heads(U_VB, B_KV_HEADS))
        ckk.append(ctx_heads(U_KC, C_HEADS))
        cvv.append(ctx_heads(U_VC, C_HEADS))
        hs.append(h_ctx)

        merged = merge_branches(h, (ya_ctx, yb_ctx, yc_ctx, yd_ctx), (ya_lat, yb_lat, yc_lat, yd_lat),
                                w_gate, w_branch[l].astype(bf16))
        x, h2 = matmul_ln(merged, w_o[l].astype(bf16), x, mods[l], GATE1, ln1_g[l][None], ln1_b[l][None],
                          mods[l], SCALE2, SHIFT2, name="out_proj_ln1")
        ff = matmul(h2, w_mlp1[l].astype(bf16), bf16, relu_sq=True, name="mlp_up")
        nxt = min(l + 1, DEPTH - 1)
        x, h = matmul_ln(ff, w_mlp2[l].astype(bf16), x, mods[l], GATE2, ln2_g[l][None], ln2_b[l][None],
                         mods[nxt], SCALE1, SHIFT1, name="mlp_down_ln2")

    y_prompt = x[:N_CTX_TOK].reshape(BATCH, SEQ, D_MODEL)
    y_sample = x[N_CTX_TOK:].reshape(DEC_BATCH, DEC_SEQ, D_MODEL)
    return (y_prompt, y_sample, jnp.stack(bk, axis=1), jnp.stack(bv, axis=1), jnp.stack(ckk, axis=1),
            jnp.stack(cvv, axis=1), jnp.stack(hs, axis=1))
```
